```python
import math
import jax, jax.numpy as jnp
from jax import lax
import numpy as np

D_MODEL = 1024
BATCH = 4
SEQ = 8192
DEPTH = 1

PLE_DIM = 256
EPS = 1e-6
GDN_HEADS = 4
GDN_DK = 128
GDN_DV = 128
CONV_K = 4
CHUNK = 64
DIFF_HEADS = 4
DIFF_DH = 64
DIFF_DV = 2 * DIFF_DH
ROT_DIM = DIFF_DH // 4
ROPE_THETA = 500000.0
Q_BLOCK = 128
D_FF = 4 * D_MODEL

GDN_QK = GDN_HEADS * GDN_DK
GDN_V = GDN_HEADS * GDN_DV
DIFF_QK = DIFF_HEADS * 2 * DIFF_DH
DIFF_V = DIFF_HEADS * DIFF_DV
IN_SIZES = (GDN_QK, GDN_QK, GDN_V, GDN_V, GDN_HEADS, GDN_HEADS, DIFF_QK, DIFF_QK, DIFF_V, D_MODEL, D_MODEL)
D_IN = int(sum(IN_SIZES))
IN_SPLITS = tuple(int(s) for s in np.cumsum(IN_SIZES)[:-1])

kernel_name = 'hybrid_gdn_diffattn_gated_merge'


def rms_norm(x, gain):
    xf = x.astype(jnp.float32)
    y = xf * lax.rsqrt(jnp.mean(xf * xf, axis=-1, keepdims=True) + EPS)
    return (y * gain.astype(jnp.float32)).astype(x.dtype)


def l2_normalize(x):
    xf = x.astype(jnp.float32)
    return xf * lax.rsqrt(jnp.sum(xf * xf, axis=-1, keepdims=True) + EPS)


def causal_depthwise_conv(x, w):
    c = x.shape[-1]
    return lax.conv_general_dilated(
        x, w[:, None, :].astype(x.dtype), window_strides=(1,), padding=[(CONV_K - 1, 0)],
        dimension_numbers=('NWC', 'WIO', 'NWC'), feature_group_count=c)


def apply_partial_rotary(x, cos, sin):
    half = ROT_DIM // 2
    c = cos[:, :, None, None, :].astype(x.dtype)
    s = sin[:, :, None, None, :].astype(x.dtype)
    x1, x2, xp = x[..., :half], x[..., half:ROT_DIM], x[..., ROT_DIM:]
    return jnp.concatenate([x1 * c - x2 * s, x2 * c + x1 * s, xp], axis=-1)


def chunked_gated_delta_rule(q, k, v, g, beta):
    b, t, h, dk = q.shape
    dv = v.shape[-1]
    n = t // CHUNK

    def to_chunks(a):
        a = a.astype(jnp.float32).reshape((b, n, CHUNK, h) + a.shape[3:])
        return jnp.moveaxis(a, 3, 1)

    q = to_chunks(q) * (dk ** -0.5)
    k = to_chunks(k)
    v = to_chunks(v)
    beta = to_chunks(beta)
    g = jnp.cumsum(to_chunks(g), axis=-1)
    idx = jnp.arange(CHUNK)
    incl = idx[:, None] >= idx[None, :]
    strict = idx[:, None] > idx[None, :]
    gdiff = g[..., :, None] - g[..., None, :]
    decay = jnp.where(incl, jnp.exp(jnp.where(incl, gdiff, 0.0)), 0.0)
    kb = k * beta[..., None]
    lower = jnp.where(strict, jnp.einsum('bhncd,bhnsd->bhncs', kb, k) * decay, 0.0)
    eye = jnp.eye(CHUNK, dtype=jnp.float32)
    tmat = lax.linalg.triangular_solve(eye + lower, jnp.broadcast_to(eye, lower.shape),
                                       left_side=True, lower=True, unit_diagonal=True)
    u = jnp.einsum('bhncs,bhnsd->bhncd', tmat, v * beta[..., None])
    w = jnp.einsum('bhncs,bhnsd->bhncd', tmat, kb * jnp.exp(g)[..., None])
    a_intra = jnp.einsum('bhncd,bhnsd->bhncs', q, k) * decay
    g_last = g[..., -1]
    q_dec = q * jnp.exp(g)[..., None]
    k_dec = k * jnp.exp(g_last[..., None] - g)[..., None]

    def step(state, xs):
        q_i, k_i, u_i, w_i, a_i, gl_i = xs
        v_new = u_i - jnp.einsum('bhck,bhkv->bhcv', w_i, state)
        o_i = jnp.einsum('bhck,bhkv->bhcv', q_i, state) + jnp.einsum('bhcs,bhsv->bhcv', a_i, v_new)
        state = state * jnp.exp(gl_i)[..., None, None] + jnp.einsum('bhck,bhcv->bhkv', k_i, v_new)
        return state, o_i

    xs = tuple(jnp.moveaxis(a, 2, 0) for a in (q_dec, k_dec, u, w, a_intra, g_last))
    s0 = jnp.zeros((b, h, dk, dv), jnp.float32)
    _, o = lax.scan(step, s0, xs)
    o = jnp.moveaxis(o, 0, 2)
    return jnp.moveaxis(o, 1, 3).reshape(b, t, h, dv)


def diff_attention(q, k, v, lam):
    b, t, h, _, dh = q.shape
    nb = t // Q_BLOCK
    qf = q.astype(jnp.float32) * (dh ** -0.5)
    qb = jnp.moveaxis(qf.reshape(b, nb, Q_BLOCK, h, 2, dh), 1, 0)
    kf = k.astype(jnp.float32)
    vf = v.astype(jnp.float32)
    key_idx = jnp.arange(t)

    def block(args):
        q_blk, blk = args
        q_idx = blk * Q_BLOCK + jnp.arange(Q_BLOCK)
        s = jnp.einsum('bqhcd,bkhcd->bhcqk', q_blk, kf)
        mask = key_idx[None, :] <= q_idx[:, None]
        pr = jax.nn.softmax(jnp.where(mask, s, -jnp.inf), axis=-1)
        a = pr[:, :, 0] - lam * pr[:, :, 1]
        return jnp.einsum('bhqk,bkhv->bqhv', a, vf)

    o = lax.map(block, (qb, jnp.arange(nb)))
    return jnp.moveaxis(o, 0, 1).reshape(b, t, h, v.shape[-1])


def setup_inputs(seed: int = 0) -> dict:
    key = jax.random.key(seed)
    ks = jax.random.split(key, 26)

    def nrm(k, shape, scale):
        return jax.random.normal(k, shape, jnp.float32) * scale

    def gain(k, shape):
        return 1.0 + 0.1 * jax.random.normal(k, shape, jnp.float32)

    offsets = jax.random.randint(ks[2], (BATCH, 1), 0, 1024, dtype=jnp.int32)
    positions = offsets + jnp.arange(SEQ, dtype=jnp.int32)[None, :]
    dt = jnp.exp(jax.random.uniform(ks[7], (DEPTH, GDN_HEADS), jnp.float32)
                 * (math.log(0.1) - math.log(0.001)) + math.log(0.001))
    dt_bias = dt + jnp.log(-jnp.expm1(-dt))
    a_log = jnp.log(jax.random.uniform(ks[6], (DEPTH, GDN_HEADS), jnp.float32, 1.0, 16.0))
    return {
        'x': nrm(ks[0], (BATCH, SEQ, D_MODEL), 1.0),
        'p': nrm(ks[1], (DEPTH, BATCH, SEQ, PLE_DIM), 1.0),
        'positions': positions,
        'attn_norm': gain(ks[3], (DEPTH, D_MODEL)),
        'w_in': nrm(ks[4], (DEPTH, D_MODEL, D_IN), D_MODEL ** -0.5),
        'conv_w': nrm(ks[5], (DEPTH, CONV_K, GDN_QK * 2 + GDN_V), CONV_K ** -0.5),
        'a_log': a_log,
        'dt_bias': dt_bias,
        'gdn_norm': gain(ks[8], (DEPTH, GDN_DV)),
        'w_o_a': nrm(ks[9], (DEPTH, GDN_V, D_MODEL), GDN_V ** -0.5),
        'q_norm': gain(ks[10], (DEPTH, DIFF_DH)),
        'k_norm': gain(ks[11], (DEPTH, DIFF_DH)),
        'lambda_q1': nrm(ks[12], (DEPTH, DIFF_DH), 0.1),
        'lambda_k1': nrm(ks[13], (DEPTH, DIFF_DH), 0.1),
        'lambda_q2': nrm(ks[14], (DEPTH, DIFF_DH), 0.1),
        'lambda_k2': nrm(ks[15], (DEPTH, DIFF_DH), 0.1),
        'diff_norm': gain(ks[16], (DEPTH, DIFF_DV)),
        'w_o_b': nrm(ks[17], (DEPTH, DIFF_V, D_MODEL), DIFF_V ** -0.5),
        'w_out': nrm(ks[18], (DEPTH, D_MODEL, D_MODEL), D_MODEL ** -0.5),
        'mlp_norm': gain(ks[19], (DEPTH, D_MODEL)),
        'w_up': nrm(ks[20], (DEPTH, D_MODEL, D_FF), D_MODEL ** -0.5),
        'w_down': nrm(ks[21], (DEPTH, D_FF, D_MODEL), D_FF ** -0.5),
        'ple_norm': gain(ks[22], (DEPTH, D_MODEL)),
        'w_ple_gate': nrm(ks[23], (DEPTH, D_MODEL, D_MODEL), D_MODEL ** -0.5),
        'w_ple': nrm(ks[24], (DEPTH, PLE_DIM, D_MODEL), PLE_DIM ** -0.5),
    }


def reference(x, p, positions, attn_norm, w_in, conv_w, a_log, dt_bias, gdn_norm, w_o_a,
              q_norm, k_norm, lambda_q1, lambda_k1, lambda_q2, lambda_k2, diff_norm, w_o_b,
              w_out, mlp_norm, w_up, w_down, ple_norm, w_ple_gate, w_ple):
    b, t, _ = x.shape
    inv_freq = ROPE_THETA ** (-jnp.arange(0, ROT_DIM, 2, dtype=jnp.float32) / ROT_DIM)
    ang = positions.astype(jnp.float32)[..., None] * inv_freq
    cos, sin = jnp.cos(ang), jnp.sin(ang)

    for i in range(DEPTH):
        h = rms_norm(x, attn_norm[i])
        proj = h @ w_in[i].astype(x.dtype)
        (a_q, a_k, a_v, a_z, a_a, a_b, b_q, b_k, b_v, gate_a, gate_b) = jnp.split(proj, IN_SPLITS, axis=-1)

        qkv = jax.nn.silu(causal_depthwise_conv(jnp.concatenate([a_q, a_k, a_v], axis=-1), conv_w[i]))
        cq, ck, cv = jnp.split(qkv, [GDN_QK, 2 * GDN_QK], axis=-1)
        cq = l2_normalize(cq.reshape(b, t, GDN_HEADS, GDN_DK))
        ck = l2_normalize(ck.reshape(b, t, GDN_HEADS, GDN_DK))
        cv = cv.reshape(b, t, GDN_HEADS, GDN_DV)
        g = -jnp.exp(a_log[i].astype(jnp.float32)) * jax.nn.softplus(
            a_a.astype(jnp.float32) + dt_bias[i].astype(jnp.float32))
        beta = jax.nn.sigmoid(a_b.astype(jnp.float32))
        o_a = chunked_gated_delta_rule(cq, ck, cv, g, beta)
        o_a = rms_norm(o_a, gdn_norm[i]) * jax.nn.silu(
            a_z.reshape(b, t, GDN_HEADS, GDN_DV).astype(jnp.float32))
        y_a = o_a.reshape(b, t, GDN_V).astype(x.dtype) @ w_o_a[i].astype(x.dtype)

        bq = rms_norm(b_q.reshape(b, t, DIFF_HEADS, 2, DIFF_DH), q_norm[i])
        bk = rms_norm(b_k.reshape(b, t, DIFF_HEADS, 2, DIFF_DH), k_norm[i])
        bq = apply_partial_rotary(bq, cos, sin)
        bk = apply_partial_rotary(bk, cos, sin)
        lam_init = 0.8 - 0.6 * math.exp(-0.3 * i)
        lam = (jnp.exp(jnp.sum(lambda_q1[i].astype(jnp.float32) * lambda_k1[i].astype(jnp.float32)))
               - jnp.exp(jnp.sum(lambda_q2[i].astype(jnp.float32) * lambda_k2[i].astype(jnp.float32)))
               + lam_init)
        o_b = diff_attention(bq, bk, b_v.reshape(b, t, DIFF_HEADS, DIFF_DV), lam)
        o_b = rms_norm(o_b, diff_norm[i]) * (1.0 - lam_init)
        y_b = o_b.reshape(b, t, DIFF_V).astype(x.dtype) @ w_o_b[i].astype(x.dtype)

        merged = jax.nn.sigmoid(gate_a) * y_a + jax.nn.sigmoid(gate_b) * y_b
        x = x + merged @ w_out[i].astype(x.dtype)

        h = rms_norm(x, mlp_norm[i])
        x = x + jnp.square(jax.nn.relu(h @ w_up[i].astype(x.dtype))) @ w_down[i].astype(x.dtype)

        gate = jax.nn.sigmoid(rms_norm(x, ple_norm[i]) @ w_ple_gate[i].astype(x.dtype))
        x = x + gate * (p[i].astype(x.dtype) @ w_ple[i].astype(x.dtype))
    return x
```

```python
import functools
import math

import jax
import jax.numpy as jnp
from jax import lax
from jax.experimental import pallas as pl
from jax.experimental.pallas import tpu as pltpu

F32 = jnp.float32
BF16 = jnp.bfloat16

D_MODEL = 1024
PLE_DIM = 256
EPS = 1e-6
GDN_HEADS = 4
GDN_DK = 128
GDN_DV = 128
CONV_K = 4
CHUNK = 64
DIFF_HEADS = 4
DIFF_DH = 64
DIFF_DV = 2 * DIFF_DH
ROT_DIM = DIFF_DH // 4
ROPE_THETA = 500000.0
D_FF = 4 * D_MODEL

GDN_QK = GDN_HEADS * GDN_DK
GDN_V = GDN_HEADS * GDN_DV
DIFF_QK = DIFF_HEADS * 2 * DIFF_DH
DIFF_V = DIFF_HEADS * DIFF_DV
D_IN = 4 * GDN_QK + 2 * GDN_HEADS + 3 * DIFF_QK + 2 * D_MODEL

COL_QKV = 0
COL_Z = 1536
COL_GATE = 2048
COL_DQ = 4096
COL_DK = 4608
COL_DV = 5120
COL_AB = 5632
N_PAD = 6144

LANE = 128
GDN_BLK = 256
CPB = GDN_BLK // CHUNK
VMEM_LIMIT = 48 * 1024 * 1024

HI = lax.Precision.HIGHEST


def _cparams(sem):
    return pltpu.CompilerParams(dimension_semantics=sem, vmem_limit_bytes=VMEM_LIMIT)


def _dot(a, b):
    return jnp.dot(a, b, preferred_element_type=F32)


def _dot_nt(a, b):
    return lax.dot_general(a, b, (((1,), (1,)), ((), ())), preferred_element_type=F32)


def _split_bf16(a):
    hi = a.astype(BF16)
    lo = (a - hi.astype(F32)).astype(BF16)
    return hi, lo


def _dot3(a, b):
    ah, al = _split_bf16(a)
    bh, bl = _split_bf16(b)
    return _dot(ah, bh) + _dot(ah, bl) + _dot(al, bh)


def _sigmoid(x):
    return 1.0 / (1.0 + jnp.exp(-x))


def _silu(x):
    return x * _sigmoid(x)


def _softplus(x):
    return jnp.maximum(x, 0.0) + jnp.log(1.0 + jnp.exp(-jnp.abs(x)))


def _inproj_kernel(x_ref, g_ref, w_ref, o_ref, h_scr):
    @pl.when(pl.program_id(1) == 0)
    def _():
        x = x_ref[...]
        ms = jnp.mean(x * x, axis=-1, keepdims=True)
        h_scr[...] = (x * lax.rsqrt(ms + EPS) * g_ref[...]).astype(BF16)

    o_ref[...] = _dot(h_scr[...], w_ref[...])


def _in_proj(x2d, gain, w_bf16, tm, tn):
    m = x2d.shape[0]
    return pl.pallas_call(
        _inproj_kernel,
        grid=(m // tm, N_PAD // tn),
        in_specs=[
            pl.BlockSpec((tm, D_MODEL), lambda i, j: (i, 0)),
            pl.BlockSpec((1, D_MODEL), lambda i, j: (0, 0)),
            pl.BlockSpec((D_MODEL, tn), lambda i, j: (0, j)),
        ],
        out_specs=pl.BlockSpec((tm, tn), lambda i, j: (i, j)),
        out_shape=jax.ShapeDtypeStruct((m, N_PAD), F32),
        scratch_shapes=[pltpu.VMEM((tm, D_MODEL), BF16)],
        compiler_params=_cparams(("parallel", "arbitrary")),
        name="in_proj",
    )(x2d, gain, w_bf16)


def _diffprep_kernel(pos_ref, freq_ref, q_ref, k_ref, v_ref, qg_ref, kg_ref, oq_ref, ok_ref, ov_ref):
    tm = q_ref.shape[0]
    ang = freq_ref[...] * pos_ref[0].astype(F32)
    cos_t, sin_t = jnp.cos(ang), jnp.sin(ang)
    half = ROT_DIM // 2
    fi = lax.broadcasted_iota(jnp.int32, (half, DIFF_QK), 0)
    li = lax.broadcasted_iota(jnp.int32, (half, DIFF_QK), 1)
    d = jnp.bitwise_and(li, DIFF_DH - 1)
    e_lo = jnp.where(d == fi, 1.0, 0.0).astype(F32)
    e_hi = jnp.where(d == fi + half, 1.0, 0.0).astype(F32)
    tdot = lambda a, b: lax.dot_general(a, b, (((0,), (0,)), ((), ())), precision=HI,
                                        preferred_element_type=F32)
    lane = lax.broadcasted_iota(jnp.int32, (1, DIFF_QK), 1)
    passthrough = jnp.where(jnp.bitwise_and(lane, DIFF_DH - 1) >= ROT_DIM, 1.0, 0.0).astype(F32)
    cos_f = tdot(cos_t, e_lo + e_hi) + passthrough
    sin_a = -tdot(sin_t, e_lo)
    sin_b = tdot(sin_t, e_hi)

    ri = lax.broadcasted_iota(jnp.int32, (DIFF_QK, DIFF_QK), 0)
    ci = lax.broadcasted_iota(jnp.int32, (DIFF_QK, DIFF_QK), 1)
    grp = jnp.where(jnp.right_shift(ri, 6) == jnp.right_shift(ci, 6), 1.0 / DIFF_DH, 0.0).astype(BF16)

    def norm_rot(x, gain, scale):
        sq = x * x
        hi, lo = _split_bf16(sq)
        ms = _dot(hi, grp) + _dot(lo, grp)
        y = x * lax.rsqrt(ms + EPS) * gain
        up = pltpu.roll(y, DIFF_QK - half, 1)
        dn = pltpu.roll(y, half, 1)
        return (y * cos_f + up * sin_a + dn * sin_b) * scale

    oq_ref[...] = norm_rot(q_ref[...], qg_ref[...], DIFF_DH ** -0.5).astype(BF16)
    ok_ref[...] = norm_rot(k_ref[...], kg_ref[...], 1.0).astype(BF16)
    ov_ref[...] = v_ref[...].astype(BF16)


def _diff_prep(proj, pos3, freq, qg, kg, tm):
    m = proj.shape[0]
    blk = lambda c: pl.BlockSpec((tm, DIFF_QK), lambda i: (i, c))
    small = lambda s: pl.BlockSpec(s, lambda i: (0,) * len(s))
    out = jax.ShapeDtypeStruct((m, DIFF_QK), BF16)
    return pl.pallas_call(
        _diffprep_kernel,
        grid=(m // tm,),
        in_specs=[
            pl.BlockSpec((1, 1, tm), lambda i: (i, 0, 0)),
            small((ROT_DIM // 2, 1)),
            blk(COL_DQ // DIFF_QK), blk(COL_DK // DIFF_QK), blk(COL_DV // DIFF_QK),
            small((1, DIFF_QK)), small((1, DIFF_QK)),
        ],
        out_specs=[pl.BlockSpec((tm, DIFF_QK), lambda i: (i, 0))] * 3,
        out_shape=[out, out, out],
        compiler_params=_cparams(("parallel",)),
        name="diff_prep",
    )(pos3, freq, proj, proj, proj, qg, kg)


def _flash_kernel(q_ref, k_ref, v_ref, lq1_ref, lk1_ref, lq2_ref, lk2_ref, gain_ref, o_ref,
                  m_scr, l_scr, acc_scr, *, tq, tk, lam_init):
    qi = pl.program_id(2)
    q = q_ref[...]
    lane = lax.broadcasted_iota(jnp.int32, q.shape, 1)
    zero = jnp.zeros_like(q)
    qs = jnp.concatenate([jnp.where(lane < DIFF_DH, q, zero), jnp.where(lane >= DIFF_DH, q, zero)], axis=0)

    m_scr[...] = jnp.full(m_scr.shape, -jnp.inf, F32)
    l_scr[...] = jnp.zeros(l_scr.shape, F32)
    acc_scr[...] = jnp.zeros(acc_scr.shape, F32)

    def step(j, masked):
        start = pl.multiple_of(j * tk, tk)
        kb = k_ref[pl.ds(start, tk), :]
        vb = v_ref[pl.ds(start, tk), :]
        s = _dot_nt(qs, kb)
        if masked:
            row = lax.broadcasted_iota(jnp.int32, s.shape, 0)
            col = lax.broadcasted_iota(jnp.int32, s.shape, 1)
            qpos = qi * tq + jnp.where(row >= tq, row - tq, row)
            s = jnp.where(start + col <= qpos, s, -jnp.inf)
        m_old = m_scr[...]
        m_new = jnp.maximum(m_old, jnp.max(s, axis=-1, keepdims=True))
        alpha = jnp.exp(m_old - m_new)
        p = jnp.exp(s - m_new)
        l_scr[...] = alpha * l_scr[...] + jnp.sum(p, axis=-1, keepdims=True)
        acc_scr[...] = alpha * acc_scr[...] + _dot(p.astype(BF16), vb)
        m_scr[...] = m_new

    r = tq // tk
    n_full = qi * r

    def body(j, c):
        step(j, False)
        return c

    lax.fori_loop(0, n_full, body, 0)
    for dj in range(r):
        step(n_full + dj, True)

    lam = (jnp.exp(jnp.sum(lq1_ref[...] * lk1_ref[...], axis=-1, keepdims=True))
           - jnp.exp(jnp.sum(lq2_ref[...] * lk2_ref[...], axis=-1, keepdims=True)) + lam_init)
    acc = acc_scr[...]
    l = l_scr[...]
    o = acc[:tq] / l[:tq] - lam * (acc[tq:] / l[tq:])
    ms = jnp.mean(o * o, axis=-1, keepdims=True)
    o_ref[...] = (o * lax.rsqrt(ms + EPS) * gain_ref[...] * (1.0 - lam_init)).astype(o_ref.dtype)


def _flash_diff(dq, dk, dv, lq1, lk1, lq2, lk2, gain, b, t, tq, tk, lam_init):
    nq = t // tq
    small = lambda s: pl.BlockSpec(s, lambda bi, h, qi: (0,) * len(s))
    kv_spec = pl.BlockSpec((t, LANE), lambda bi, h, qi: (bi, h))
    return pl.pallas_call(
        functools.partial(_flash_kernel, tq=tq, tk=tk, lam_init=lam_init),
        grid=(b, DIFF_HEADS, nq),
        in_specs=[
            pl.BlockSpec((tq, LANE), lambda bi, h, qi: (bi * nq + qi, h)),
            kv_spec, kv_spec,
            small((1, DIFF_DH)), small((1, DIFF_DH)), small((1, DIFF_DH)), small((1, DIFF_DH)),
            small((1, DIFF_DV)),
        ],
        out_specs=pl.BlockSpec((tq, LANE), lambda bi, h, qi: (bi * nq + qi, h)),
        out_shape=jax.ShapeDtypeStruct((b * t, DIFF_V), BF16),
        scratch_shapes=[pltpu.VMEM((2 * tq, 1), F32), pltpu.VMEM((2 * tq, 1), F32),
                        pltpu.VMEM((2 * tq, DIFF_DV), F32)],
        compiler_params=_cparams(("parallel", "parallel", "arbitrary")),
        name="flash_diff",
    )(dq, dk, dv, lq1, lk1, lq2, lk2, gain)


def _gdnprep_kernel(cur_ref, halo_ref, ab_ref, convw_ref, alog_ref, dtb_ref, alogc_ref, dtbc_ref,
                    wq_ref, u_ref, aq_ref, kdt_ref, eg_ref, *, nblk):
    i = pl.program_id(0)
    n = GDN_BLK
    keep = jnp.where(i % nblk == 0, 0.0, 1.0).astype(F32)
    xs = jnp.concatenate([halo_ref[...] * keep, cur_ref[...]], axis=0)
    cw = convw_ref[...]
    conv = xs[8:8 + n] * cw[CONV_K - 1:CONV_K]
    for j in range(CONV_K - 1):
        off = 8 - (CONV_K - 1) + j
        conv = conv + xs[off:off + n] * cw[j:j + 1]
    qkv = _silu(conv)

    ab = ab_ref[...]
    ab_t = ab.T
    g_col = -jnp.exp(alog_ref[...]) * _softplus(ab + dtb_ref[...])
    beta_col = _sigmoid(ab)
    g_row = -jnp.exp(alogc_ref[...]) * _softplus(ab_t[:8] + dtbc_ref[...])

    ri = lax.broadcasted_iota(jnp.int32, (n, n), 0)
    ci = lax.broadcasted_iota(jnp.int32, (n, n), 1)
    same = jnp.right_shift(ri, 6) == jnp.right_shift(ci, 6)
    incl = jnp.logical_and(same, ri >= ci)
    strict = jnp.logical_and(same, ri > ci)
    one = jnp.ones((n, n), F32)
    zero = jnp.zeros((n, n), F32)
    tri = jnp.where(incl, one, zero)
    blk_ones = jnp.where(same, one, zero)
    eye = jnp.where(ri == ci, one, zero)
    hdot = lambda a, b: jnp.dot(a, b, precision=HI, preferred_element_type=F32)
    gc_col = hdot(tri, g_col)
    gl_col = hdot(blk_ones, g_col)
    gc_row = lax.dot_general(g_row, tri, (((1,), (1,)), ((), ())), precision=HI,
                             preferred_element_type=F32)

    for h in range(GDN_HEADS):
        sl = slice(h * GDN_DK, (h + 1) * GDN_DK)
        q = qkv[:, sl]
        k = qkv[:, GDN_QK + h * GDN_DK: GDN_QK + (h + 1) * GDN_DK]
        v = qkv[:, 2 * GDN_QK + h * GDN_DV: 2 * GDN_QK + (h + 1) * GDN_DV]
        q = q * lax.rsqrt(jnp.sum(q * q, axis=-1, keepdims=True) + EPS) * (GDN_DK ** -0.5)
        k = k * lax.rsqrt(jnp.sum(k * k, axis=-1, keepdims=True) + EPS)
        beta = beta_col[:, GDN_HEADS + h: GDN_HEADS + h + 1]
        gcb = jnp.broadcast_to(gc_col[:, h:h + 1], (n, GDN_DK))
        glb = jnp.broadcast_to(gl_col[:, h:h + 1], (n, GDN_DK))
        gdiff = jnp.concatenate([gcb, gcb], axis=1) - gc_row[h:h + 1, :]
        decay = jnp.where(incl, jnp.exp(jnp.where(incl, gdiff, zero)), zero)
        kb = k * beta
        kbf = k.astype(BF16)
        lower = jnp.where(strict, _dot_nt(kb.astype(BF16), kbf) * decay, zero)
        x = -lower
        tmat = eye + x
        for _ in range(int(math.log2(CHUNK)) - 1):
            x = _dot3(x, x)
            tmat = tmat + _dot3(tmat, x)
        egc = jnp.exp(gcb)
        rhs = jnp.concatenate([v * beta, kb * egc], axis=1).astype(BF16)
        uw = _dot(tmat.astype(BF16), rhs)
        u = uw[:, :GDN_DV]
        w = uw[:, GDN_DV:]
        a_intra = (_dot_nt(q.astype(BF16), kbf) * decay).astype(BF16)
        q_dec = (q * egc).astype(BF16)
        kd_t = (k * jnp.exp(glb - gcb)).T.astype(BF16)
        eg = jnp.exp(glb)
        for c in range(CPB):
            rs = slice(c * CHUNK, (c + 1) * CHUNK)
            wq_ref[0, h, c, 0:CHUNK, :] = w[rs].astype(BF16)
            wq_ref[0, h, c, CHUNK:2 * CHUNK, :] = q_dec[rs]
            u_ref[0, h, c] = u[rs]
            aq_ref[0, h, c] = a_intra[rs, rs]
            kdt_ref[0, h, c] = kd_t[:, rs]
            eg_ref[0, h, c] = eg[c * CHUNK: c * CHUNK + 1, :]


def _gdn_prep(proj, conv_w, a_log_pat, dtb_pat, a_log_col, dtb_col, b, t):
    nblk = t // GDN_BLK
    nc = t // CHUNK
    c3 = 2 * GDN_QK + GDN_V
    small = lambda s: pl.BlockSpec(s, lambda i: (0,) * len(s))
    omap = lambda i: (i // nblk, 0, i % nblk, 0, 0)
    return pl.pallas_call(
        functools.partial(_gdnprep_kernel, nblk=nblk),
        grid=(b * nblk,),
        in_specs=[
            pl.BlockSpec((GDN_BLK, c3), lambda i: (i, COL_QKV // c3)),
            pl.BlockSpec((8, c3), lambda i: (jnp.maximum(i * (GDN_BLK // 8) - 1, 0), COL_QKV // c3)),
            pl.BlockSpec((GDN_BLK, LANE), lambda i: (i, COL_AB // LANE)),
            small((CONV_K, c3)), small((1, LANE)), small((1, LANE)), small((8, 1)), small((8, 1)),
        ],
        out_specs=[
            pl.BlockSpec((1, GDN_HEADS, CPB, 2 * CHUNK, GDN_DK), omap),
            pl.BlockSpec((1, GDN_HEADS, CPB, CHUNK, GDN_DV), omap),
            pl.BlockSpec((1, GDN_HEADS, CPB, CHUNK, CHUNK), omap),
            pl.BlockSpec((1, GDN_HEADS, CPB, GDN_DK, CHUNK), omap),
            pl.BlockSpec((1, GDN_HEADS, CPB, 1, GDN_DV), omap),
        ],
        out_shape=[
            jax.ShapeDtypeStruct((b, GDN_HEADS, nc, 2 * CHUNK, GDN_DK), BF16),
            jax.ShapeDtypeStruct((b, GDN_HEADS, nc, CHUNK, GDN_DV), F32),
            jax.ShapeDtypeStruct((b, GDN_HEADS, nc, CHUNK, CHUNK), BF16),
            jax.ShapeDtypeStruct((b, GDN_HEADS, nc, GDN_DK, CHUNK), BF16),
            jax.ShapeDtypeStruct((b, GDN_HEADS, nc, 1, GDN_DV), F32),
        ],
        compiler_params=_cparams(("parallel",)),
        name="gdn_prep",
    )(proj, proj, proj, conv_w, a_log_pat, dtb_pat, a_log_col, dtb_col)


def _gdnscan_kernel(wq_ref, u_ref, aq_ref, kdt_ref, eg_ref, o_ref, s_scr, *, nb, cb):
    @pl.when(pl.program_id(0) == 0)
    def _():
        s_scr[...] = jnp.zeros(s_scr.shape, F32)

    def chunk(c, carry):
        for bi in range(nb):
            for h in range(GDN_HEADS):
                s = s_scr[bi, h]
                r = _dot(wq_ref[bi, h, c], s.astype(BF16))
                v_new = u_ref[bi, h, c] - r[:CHUNK]
                vb = v_new.astype(BF16)
                o_ref[bi, h, c] = r[CHUNK:] + _dot(aq_ref[bi, h, c], vb)
                s_scr[bi, h] = s * eg_ref[bi, h, c] + _dot(kdt_ref[bi, h, c], vb)
        return carry

    lax.fori_loop(0, cb, chunk, 0)


def _gdn_scan(wq, u, aq, kdt, eg, cb):
    b, _, nc = wq.shape[:3]
    spec = lambda d0, d1: pl.BlockSpec((b, GDN_HEADS, cb, d0, d1), lambda i: (0, 0, i, 0, 0))
    return pl.pallas_call(
        functools.partial(_gdnscan_kernel, nb=b, cb=cb),
        grid=(nc // cb,),
        in_specs=[spec(2 * CHUNK, GDN_DK), spec(CHUNK, GDN_DV), spec(CHUNK, CHUNK),
                  spec(GDN_DK, CHUNK), spec(1, GDN_DV)],
        out_specs=spec(CHUNK, GDN_DV),
        out_shape=jax.ShapeDtypeStruct((b, GDN_HEADS, nc, CHUNK, GDN_DV), F32),
        scratch_shapes=[pltpu.VMEM((b, GDN_HEADS, GDN_DK, GDN_DV), F32)],
        compiler_params=_cparams(("arbitrary",)),
        name="gdn_scan",
    )(wq, u, aq, kdt, eg)


def _merge_kernel(x_ref, oa_ref, z_ref, ob_ref, gate_ref, gn_ref, woa_ref, wob_ref, wout_ref, o_ref):
    parts = []
    for h in range(GDN_HEADS):
        o = oa_ref[0, h]
        ms = jnp.mean(o * o, axis=-1, keepdims=True)
        z = z_ref[:, h * GDN_DV:(h + 1) * GDN_DV]
        parts.append((o * lax.rsqrt(ms + EPS) * gn_ref[...] * _silu(z)).astype(BF16))
    y_a = _dot(jnp.concatenate(parts, axis=1), woa_ref[...])
    y_b = _dot(ob_ref[...], wob_ref[...])
    g = gate_ref[...]
    merged = _sigmoid(g[:, :D_MODEL]) * y_a + _sigmoid(g[:, D_MODEL:]) * y_b
    o_ref[...] = x_ref[...] + _dot(merged.astype(BF16), wout_ref[...])


def _merge(x2d, o_a, proj, o_b, gdn_gain, w_o_a, w_o_b, w_out, t, tm):
    m = x2d.shape[0]
    npb = t // tm
    small = lambda s: pl.BlockSpec(s, lambda i: (0,) * len(s))
    return pl.pallas_call(
        _merge_kernel,
        grid=(m // tm,),
        in_specs=[
            pl.BlockSpec((tm, D_MODEL), lambda i: (i, 0)),
            pl.BlockSpec((1, GDN_HEADS, tm, GDN_DV), lambda i: (i // npb, 0, i % npb, 0)),
            pl.BlockSpec((tm, GDN_V), lambda i: (i, COL_Z // GDN_V)),
            pl.BlockSpec((tm, DIFF_V), lambda i: (i, 0)),
            pl.BlockSpec((tm, 2 * D_MODEL), lambda i: (i, COL_GATE // (2 * D_MODEL))),
            small((1, GDN_DV)), small((GDN_V, D_MODEL)), small((DIFF_V, D_MODEL)), small((D_MODEL, D_MODEL)),
        ],
        out_specs=pl.BlockSpec((tm, D_MODEL), lambda i: (i, 0)),
        out_shape=jax.ShapeDtypeStruct((m, D_MODEL), F32),
        compiler_params=_cparams(("parallel",)),
        name="merge",
    )(x2d, o_a, proj, o_b, proj, gdn_gain, w_o_a, w_o_b, w_out)


def _mlp_kernel(x_ref, mg_ref, wup_ref, wdn_ref, pg_ref, wpg_ref, p_ref, wp_ref, o_ref, h_scr, acc_scr):
    f = pl.program_id(1)

    @pl.when(f == 0)
    def _():
        x = x_ref[...]
        ms = jnp.mean(x * x, axis=-1, keepdims=True)
        h_scr[...] = (x * lax.rsqrt(ms + EPS) * mg_ref[...]).astype(BF16)
        acc_scr[...] = jnp.zeros(acc_scr.shape, F32)

    a = jnp.maximum(_dot(h_scr[...], wup_ref[...]), 0.0)
    acc_scr[...] += _dot((a * a).astype(BF16), wdn_ref[...])

    @pl.when(f == pl.num_programs(1) - 1)
    def _():
        x2 = x_ref[...] + acc_scr[...]
        ms = jnp.mean(x2 * x2, axis=-1, keepdims=True)
        hn = (x2 * lax.rsqrt(ms + EPS) * pg_ref[...]).astype(BF16)
        gate = _sigmoid(_dot(hn, wpg_ref[...]))
        o_ref[...] = x2 + gate * _dot(p_ref[...].astype(BF16), wp_ref[...])


def _mlp_ple(x1, mlp_gain, w_up, w_down, ple_gain, w_ple_gate, p2d, w_ple, tm, tf):
    m = x1.shape[0]
    small = lambda s: pl.BlockSpec(s, lambda i, f: (0,) * len(s))
    return pl.pallas_call(
        _mlp_kernel,
        grid=(m // tm, D_FF // tf),
        in_specs=[
            pl.BlockSpec((tm, D_MODEL), lambda i, f: (i, 0)),
            small((1, D_MODEL)),
            pl.BlockSpec((D_MODEL, tf), lambda i, f: (0, f)),
            pl.BlockSpec((tf, D_MODEL), lambda i, f: (f, 0)),
            small((1, D_MODEL)), small((D_MODEL, D_MODEL)),
            pl.BlockSpec((tm, PLE_DIM), lambda i, f: (i, 0)),
            small((PLE_DIM, D_MODEL)),
        ],
        out_specs=pl.BlockSpec((tm, D_MODEL), lambda i, f: (i, 0)),
        out_shape=jax.ShapeDtypeStruct((m, D_MODEL), F32),
        scratch_shapes=[pltpu.VMEM((tm, D_MODEL), BF16), pltpu.VMEM((tm, D_MODEL), F32)],
        compiler_params=_cparams(("parallel", "arbitrary")),
        name="mlp_ple",
    )(x1, mlp_gain, w_up, w_down, ple_gain, w_ple_gate, p2d, w_ple)


def _reorder_w_in(w):
    o_z = 3 * GDN_QK
    o_a = o_z + GDN_V
    o_dq = o_a + 2 * GDN_HEADS
    o_gate = o_dq + 3 * DIFF_QK
    pad = jnp.zeros((D_MODEL, N_PAD - COL_AB - 2 * GDN_HEADS), w.dtype)
    return jnp.concatenate(
        [w[:, :o_a], w[:, o_gate:], w[:, o_dq:o_gate], w[:, o_a:o_dq], pad], axis=1).astype(BF16)


def _lane_pat(vec, offset):
    return jnp.zeros((1, LANE), F32).at[0, offset:offset + GDN_HEADS].set(vec.astype(F32))


def _col_pat(vec):
    return jnp.zeros((8, 1), F32).at[:GDN_HEADS, 0].set(vec.astype(F32))


def _pick(n, pref):
    return pref if n % pref == 0 else n


def kernel(x, p, positions, attn_norm, w_in, conv_w, a_log, dt_bias, gdn_norm, w_o_a, q_norm, k_norm,
           lambda_q1, lambda_k1, lambda_q2, lambda_k2, diff_norm, w_o_b, w_out, mlp_norm, w_up, w_down,
           ple_norm, w_ple_gate, w_ple):
    b, t, _ = x.shape
    m = b * t
    depth = w_in.shape[0]
    assert t % GDN_BLK == 0, "sequence length must be a multiple of the gdn_prep block"

    tm_big = _pick(m, 1024)
    tm_mid = _pick(m, 512)
    tq = _pick(t, 256)
    cb = _pick(t // CHUNK, 4)
    pos3 = positions.reshape(m // tm_mid, 1, tm_mid)
    freq = (ROPE_THETA ** (-jnp.arange(0, ROT_DIM, 2, dtype=F32) / ROT_DIM)).reshape(ROT_DIM // 2, 1)
    row = lambda v: v.astype(F32).reshape(1, -1)

    x2d = x.reshape(m, D_MODEL)
    for i in range(depth):
        lam_init = 0.8 - 0.6 * math.exp(-0.3 * i)
        proj = _in_proj(x2d, row(attn_norm[i]), _reorder_w_in(w_in[i]), tm_big, 1024)

        dq, dk, dv = _diff_prep(proj, pos3, freq, jnp.tile(row(q_norm[i]), (1, DIFF_QK // DIFF_DH)),
                                jnp.tile(row(k_norm[i]), (1, DIFF_QK // DIFF_DH)), tm_mid)
        o_b = _flash_diff(dq, dk, dv, row(lambda_q1[i]), row(lambda_k1[i]), row(lambda_q2[i]),
                          row(lambda_k2[i]), row(diff_norm[i]), b, t, tq, tq, lam_init)

        wq, u, aq, kdt, eg = _gdn_prep(proj, conv_w[i].astype(F32), _lane_pat(a_log[i], 0),
                                       _lane_pat(dt_bias[i], 0), _col_pat(a_log[i]), _col_pat(dt_bias[i]), b, t)
        o_a = _gdn_scan(wq, u, aq, kdt, eg, cb).reshape(b, GDN_HEADS, t, GDN_DV)

        x1 = _merge(x2d, o_a, proj, o_b, row(gdn_norm[i]), w_o_a[i].astype(BF16), w_o_b[i].astype(BF16),
                    w_out[i].astype(BF16), t, _pick(t, 512))
        x2d = _mlp_ple(x1, row(mlp_norm[i]), w_up[i].astype(BF16), w_down[i].astype(BF16), row(ple_norm[i]),
                       w_ple_gate[i].astype(BF16), p[i].reshape(m, PLE_DIM), w_ple[i].astype(BF16),
                       tm_mid, 1024)
    return x2d.reshape(b, t, D_MODEL)
```

```python
import functools
import math

import jax
import jax.numpy as jnp
from jax import lax
from jax.experimental import pallas as pl
from jax.experimental.pallas import tpu as pltpu

F32 = jnp.float32
BF16 = jnp.bfloat16

D_MODEL = 1024
PLE_DIM = 256
EPS = 1e-6
GDN_HEADS = 4
GDN_DK = 128
GDN_DV = 128
CONV_K = 4
CHUNK = 64
DIFF_HEADS = 4
DIFF_DH = 64
DIFF_DV = 2 * DIFF_DH
ROT_DIM = DIFF_DH // 4
ROPE_THETA = 500000.0
D_FF = 4 * D_MODEL

GDN_QK = GDN_HEADS * GDN_DK
GDN_V = GDN_HEADS * GDN_DV
DIFF_QK = DIFF_HEADS * 2 * DIFF_DH
DIFF_V = DIFF_HEADS * DIFF_DV
D_IN = 4 * GDN_QK + 2 * GDN_HEADS + 3 * DIFF_QK + 2 * D_MODEL

COL_QKV = 0
COL_Z = 1536
COL_GATE = 2048
COL_DQ = 4096
COL_DK = 4608
COL_DV = 5120
COL_AB = 5632
N_PAD = 6144

LANE = 128
GDN_BLK = 256
CPB = GDN_BLK // CHUNK
VMEM_LIMIT = 48 * 1024 * 1024

HI = lax.Precision.HIGHEST


def _cparams(sem):
    return pltpu.CompilerParams(dimension_semantics=sem, vmem_limit_bytes=VMEM_LIMIT)


def _dot(a, b):
    return jnp.dot(a, b, preferred_element_type=F32)


def _dot_nt(a, b):
    return lax.dot_general(a, b, (((1,), (1,)), ((), ())), preferred_element_type=F32)


def _split_bf16(a):
    hi = a.astype(BF16)
    lo = (a - hi.astype(F32)).astype(BF16)
    return hi, lo


def _dot3(a, b):
    ah, al = _split_bf16(a)
    bh, bl = _split_bf16(b)
    return _dot(ah, bh) + _dot(ah, bl) + _dot(al, bh)


def _sigmoid(x):
    return 1.0 / (1.0 + jnp.exp(-x))


def _silu(x):
    return x * _sigmoid(x)


def _softplus(x):
    return jnp.maximum(x, 0.0) + jnp.log(1.0 + jnp.exp(-jnp.abs(x)))


def _inproj_kernel(x_ref, g_ref, w_ref, o_ref, h_scr):
    @pl.when(pl.program_id(1) == 0)
    def _():
        x = x_ref[...]
        ms = jnp.mean(x * x, axis=-1, keepdims=True)
        h_scr[...] = (x * lax.rsqrt(ms + EPS) * g_ref[...]).astype(BF16)

    o_ref[...] = _dot(h_scr[...], w_ref[...])


def _in_proj(x2d, gain, w_bf16, tm, tn):
    m = x2d.shape[0]
    return pl.pallas_call(
        _inproj_kernel,
        grid=(m // tm, N_PAD // tn),
        in_specs=[
            pl.BlockSpec((tm, D_MODEL), lambda i, j: (i, 0)),
            pl.BlockSpec((1, D_MODEL), lambda i, j: (0, 0)),
            pl.BlockSpec((D_MODEL, tn), lambda i, j: (0, j)),
        ],
        out_specs=pl.BlockSpec((tm, tn), lambda i, j: (i, j)),
        out_shape=jax.ShapeDtypeStruct((m, N_PAD), F32),
        scratch_shapes=[pltpu.VMEM((tm, D_MODEL), BF16)],
        compiler_params=_cparams(("parallel", "arbitrary")),
        name="in_proj",
    )(x2d, gain, w_bf16)


def _diffprep_kernel(pos_ref, freq_ref, q_ref, k_ref, v_ref, qg_ref, kg_ref, oq_ref, ok_ref, ov_ref):
    tm = q_ref.shape[0]
    ang = freq_ref[...] * pos_ref[0].astype(F32)
    cos_t, sin_t = jnp.cos(ang), jnp.sin(ang)
    half = ROT_DIM // 2
    fi = lax.broadcasted_iota(jnp.int32, (half, DIFF_QK), 0)
    li = lax.broadcasted_iota(jnp.int32, (half, DIFF_QK), 1)
    d = jnp.bitwise_and(li, DIFF_DH - 1)
    e_lo = jnp.where(d == fi, 1.0, 0.0).astype(F32)
    e_hi = jnp.where(d == fi + half, 1.0, 0.0).astype(F32)
    tdot = lambda a, b: lax.dot_general(a, b, (((0,), (0,)), ((), ())), precision=HI,
                                        preferred_element_type=F32)
    lane = lax.broadcasted_iota(jnp.int32, (1, DIFF_QK), 1)
    passthrough = jnp.where(jnp.bitwise_and(lane, DIFF_DH - 1) >= ROT_DIM, 1.0, 0.0).astype(F32)
    cos_f = tdot(cos_t, e_lo + e_hi) + passthrough
    sin_a = -tdot(sin_t, e_lo)
    sin_b = tdot(sin_t, e_hi)

    ri = lax.broadcasted_iota(jnp.int32, (DIFF_QK, DIFF_QK), 0)
    ci = lax.broadcasted_iota(jnp.int32, (DIFF_QK, DIFF_QK), 1)
    grp = jnp.where(jnp.right_shift(ri, 6) == jnp.right_shift(ci, 6), 1.0 / DIFF_DH, 0.0).astype(BF16)

    def norm_rot(x, gain, scale):
        sq = x * x
        hi, lo = _split_bf16(sq)
        ms = _dot(hi, grp) + _dot(lo, grp)
        y = x * lax.rsqrt(ms + EPS) * gain
        up = pltpu.roll(y, DIFF_QK - half, 1)
        dn = pltpu.roll(y, half, 1)
        return (y * cos_f + up * sin_a + dn * sin_b) * scale

    oq_ref[0] = norm_rot(q_ref[...], qg_ref[...], DIFF_DH ** -0.5).T.astype(BF16)
    ok_ref[...] = norm_rot(k_ref[...], kg_ref[...], 1.0).astype(BF16)
    ov_ref[0] = v_ref[...].T.astype(BF16)


def _diff_prep(proj, pos3, freq, qg, kg, tm):
    m = proj.shape[0]
    blk = lambda c: pl.BlockSpec((tm, DIFF_QK), lambda i: (i, c))
    small = lambda s: pl.BlockSpec(s, lambda i: (0,) * len(s))
    out_t = jax.ShapeDtypeStruct((m // tm, DIFF_QK, tm), BF16)
    spec_t = pl.BlockSpec((1, DIFF_QK, tm), lambda i: (i, 0, 0))
    return pl.pallas_call(
        _diffprep_kernel,
        grid=(m // tm,),
        in_specs=[
            pl.BlockSpec((1, 1, tm), lambda i: (i, 0, 0)),
            small((ROT_DIM // 2, 1)),
            blk(COL_DQ // DIFF_QK), blk(COL_DK // DIFF_QK), blk(COL_DV // DIFF_QK),
            small((1, DIFF_QK)), small((1, DIFF_QK)),
        ],
        out_specs=[spec_t, pl.BlockSpec((tm, DIFF_QK), lambda i: (i, 0)), spec_t],
        out_shape=[out_t, jax.ShapeDtypeStruct((m, DIFF_QK), BF16), out_t],
        compiler_params=_cparams(("parallel",)),
        name="diff_prep",
    )(pos3, freq, proj, proj, proj, qg, kg)


def _flash_kernel(qt_ref, k_ref, vt_ref, lq1_ref, lk1_ref, lq2_ref, lk2_ref, gain_ref, o_ref,
                  acc_scr, *, tq, tk, lam_init):
    qi = pl.program_id(2)
    qt = qt_ref[0]
    row = lax.broadcasted_iota(jnp.int32, qt.shape, 0)
    zero = jnp.zeros_like(qt)
    qst = jnp.concatenate([jnp.where(row < DIFF_DH, qt, zero), jnp.where(row >= DIFF_DH, qt, zero)], axis=1)

    acc_scr[...] = jnp.zeros(acc_scr.shape, F32)

    def step(j, m_old, l_old, masked):
        start = pl.multiple_of(j * tk, tk)
        s = _dot(k_ref[pl.ds(start, tk), :], qst)
        if masked:
            r = lax.broadcasted_iota(jnp.int32, s.shape, 0)
            c = lax.broadcasted_iota(jnp.int32, s.shape, 1)
            qpos = qi * tq + jnp.where(c >= tq, c - tq, c)
            s = jnp.where(start + r <= qpos, s, -jnp.inf)
        m_new = jnp.maximum(m_old, jnp.max(s, axis=0, keepdims=True))
        alpha = jnp.exp(m_old - m_new)
        p = jnp.exp(s - m_new)
        l_new = alpha * l_old + jnp.sum(p, axis=0, keepdims=True)
        acc_scr[...] = alpha * acc_scr[...] + _dot(vt_ref[j], p.astype(BF16))
        return m_new, l_new

    n_full = (qi * tq) // tk
    m0 = jnp.full((1, 2 * tq), -jnp.inf, F32)
    l0 = jnp.zeros((1, 2 * tq), F32)
    m, l = lax.fori_loop(0, n_full, lambda j, ml: step(j, ml[0], ml[1], False), (m0, l0))
    m, l = step(n_full, m, l, True)

    lam = (jnp.exp(jnp.sum(lq1_ref[...] * lk1_ref[...], axis=-1, keepdims=True))
           - jnp.exp(jnp.sum(lq2_ref[...] * lk2_ref[...], axis=-1, keepdims=True)) + lam_init)
    acc = acc_scr[...] / l
    o = (acc[:, :tq] - lam * acc[:, tq:]).T
    ms = jnp.mean(o * o, axis=-1, keepdims=True)
    o_ref[...] = (o * lax.rsqrt(ms + EPS) * gain_ref[...] * (1.0 - lam_init)).astype(o_ref.dtype)


def _flash_diff(dqt, dk, dvt, lq1, lk1, lq2, lk2, gain, b, t, tq, tk, lam_init):
    nq = t // tq
    nkb = t // tk
    qpb = tk // tq
    small = lambda s: pl.BlockSpec(s, lambda bi, h, qi: (0,) * len(s))
    return pl.pallas_call(
        functools.partial(_flash_kernel, tq=tq, tk=tk, lam_init=lam_init),
        grid=(b, DIFF_HEADS, nq),
        in_specs=[
            pl.BlockSpec((1, LANE, tq), lambda bi, h, qi: (bi * nkb + qi // qpb, h, qi % qpb)),
            pl.BlockSpec((t, LANE), lambda bi, h, qi: (bi, h)),
            pl.BlockSpec((nkb, LANE, tk), lambda bi, h, qi: (bi, h, 0)),
            small((1, DIFF_DH)), small((1, DIFF_DH)), small((1, DIFF_DH)), small((1, DIFF_DH)),
            small((1, DIFF_DV)),
        ],
        out_specs=pl.BlockSpec((tq, LANE), lambda bi, h, qi: (bi * nq + qi, h)),
        out_shape=jax.ShapeDtypeStruct((b * t, DIFF_V), BF16),
        scratch_shapes=[pltpu.VMEM((DIFF_DV, 2 * tq), F32)],
        compiler_params=_cparams(("parallel", "parallel", "arbitrary")),
        name="flash_diff",
    )(dqt, dk, dvt, lq1, lk1, lq2, lk2, gain)


def _gdnprep_kernel(cur_ref, halo_ref, ab_ref, convw_ref, alog_ref, dtb_ref, alogc_ref, dtbc_ref,
                    wq_ref, u_ref, aq_ref, kdt_ref, eg_ref, *, nblk):
    i = pl.program_id(0)
    n = GDN_BLK
    keep = jnp.where(i % nblk == 0, 0.0, 1.0).astype(F32)
    xs = jnp.concatenate([halo_ref[...] * keep, cur_ref[...]], axis=0)
    cw = convw_ref[...]
    conv = xs[8:8 + n] * cw[CONV_K - 1:CONV_K]
    for j in range(CONV_K - 1):
        off = 8 - (CONV_K - 1) + j
        conv = conv + xs[off:off + n] * cw[j:j + 1]
    qkv = _silu(conv)

    ab = ab_ref[...]
    ab_t = ab.T
    g_col = -jnp.exp(alog_ref[...]) * _softplus(ab + dtb_ref[...])
    beta_col = _sigmoid(ab)
    g_row = -jnp.exp(alogc_ref[...]) * _softplus(ab_t[:8] + dtbc_ref[...])

    ri = lax.broadcasted_iota(jnp.int32, (n, n), 0)
    ci = lax.broadcasted_iota(jnp.int32, (n, n), 1)
    same = jnp.right_shift(ri, 6) == jnp.right_shift(ci, 6)
    incl = jnp.logical_and(same, ri >= ci)
    strict = jnp.logical_and(same, ri > ci)
    one = jnp.ones((n, n), F32)
    zero = jnp.zeros((n, n), F32)
    tri = jnp.where(incl, one, zero)
    blk_ones = jnp.where(same, one, zero)
    eye = jnp.where(ri == ci, one, zero)
    hdot = lambda a, b: jnp.dot(a, b, precision=HI, preferred_element_type=F32)
    gc_col = hdot(tri, g_col)
    gl_col = hdot(blk_ones, g_col)
    gc_row = lax.dot_general(g_row, tri, (((1,), (1,)), ((), ())), precision=HI,
                             preferred_element_type=F32)

    for h in range(GDN_HEADS):
        sl = slice(h * GDN_DK, (h + 1) * GDN_DK)
        q = qkv[:, sl]
        k = qkv[:, GDN_QK + h * GDN_DK: GDN_QK + (h + 1) * GDN_DK]
        v = qkv[:, 2 * GDN_QK + h * GDN_DV: 2 * GDN_QK + (h + 1) * GDN_DV]
        q = q * lax.rsqrt(jnp.sum(q * q, axis=-1, keepdims=True) + EPS) * (GDN_DK ** -0.5)
        k = k * lax.rsqrt(jnp.sum(k * k, axis=-1, keepdims=True) + EPS)
        beta = beta_col[:, GDN_HEADS + h: GDN_HEADS + h + 1]
        gcb = jnp.broadcast_to(gc_col[:, h:h + 1], (n, GDN_DK))
        glb = jnp.broadcast_to(gl_col[:, h:h + 1], (n, GDN_DK))
        gdiff = jnp.concatenate([gcb, gcb], axis=1) - gc_row[h:h + 1, :]
        decay = jnp.where(incl, jnp.exp(jnp.where(incl, gdiff, zero)), zero)
        kb = k * beta
        kbf = k.astype(BF16)
        lower = jnp.where(strict, _dot_nt(kb.astype(BF16), kbf) * decay, zero)
        x = -lower
        tmat = eye + x
        for _ in range(int(math.log2(CHUNK)) - 1):
            x = _dot3(x, x)
            tmat = tmat + _dot3(tmat, x)
        egc = jnp.exp(gcb)
        rhs = jnp.concatenate([v * beta, kb * egc], axis=1).astype(BF16)
        uw = _dot(tmat.astype(BF16), rhs)
        u = uw[:, :GDN_DV]
        w = uw[:, GDN_DV:]
        a_intra = (_dot_nt(q.astype(BF16), kbf) * decay).astype(BF16)
        q_dec = (q * egc).astype(BF16)
        kd_t = (k * jnp.exp(glb - gcb)).T.astype(BF16)
        eg = jnp.exp(glb)
        for c in range(CPB):
            rs = slice(c * CHUNK, (c + 1) * CHUNK)
            wq_ref[0, h, c, 0:CHUNK, :] = w[rs].astype(BF16)
            wq_ref[0, h, c, CHUNK:2 * CHUNK, :] = q_dec[rs]
            u_ref[0, h, c] = u[rs]
            aq_ref[0, h, c] = a_intra[rs, rs]
            kdt_ref[0, h, c] = kd_t[:, rs]
            eg_ref[0, h, c] = eg[c * CHUNK: c * CHUNK + 1, :]


def _gdn_prep(proj, conv_w, a_log_pat, dtb_pat, a_log_col, dtb_col, b, t):
    nblk = t // GDN_BLK
    nc = t // CHUNK
    c3 = 2 * GDN_QK + GDN_V
    small = lambda s: pl.BlockSpec(s, lambda i: (0,) * len(s))
    omap = lambda i: (i // nblk, 0, i % nblk, 0, 0)
    return pl.pallas_call(
        functools.partial(_gdnprep_kernel, nblk=nblk),
        grid=(b * nblk,),
        in_specs=[
            pl.BlockSpec((GDN_BLK, c3), lambda i: (i, COL_QKV // c3)),
            pl.BlockSpec((8, c3), lambda i: (jnp.maximum(i * (GDN_BLK // 8) - 1, 0), COL_QKV // c3)),
            pl.BlockSpec((GDN_BLK, LANE), lambda i: (i, COL_AB // LANE)),
            small((CONV_K, c3)), small((1, LANE)), small((1, LANE)), small((8, 1)), small((8, 1)),
        ],
        out_specs=[
            pl.BlockSpec((1, GDN_HEADS, CPB, 2 * CHUNK, GDN_DK), omap),
            pl.BlockSpec((1, GDN_HEADS, CPB, CHUNK, GDN_DV), omap),
            pl.BlockSpec((1, GDN_HEADS, CPB, CHUNK, CHUNK), omap),
            pl.BlockSpec((1, GDN_HEADS, CPB, GDN_DK, CHUNK), omap),
            pl.BlockSpec((1, GDN_HEADS, CPB, 1, GDN_DV), omap),
        ],
        out_shape=[
            jax.ShapeDtypeStruct((b, GDN_HEADS, nc, 2 * CHUNK, GDN_DK), BF16),
            jax.ShapeDtypeStruct((b, GDN_HEADS, nc, CHUNK, GDN_DV), F32),
            jax.ShapeDtypeStruct((b, GDN_HEADS, nc, CHUNK, CHUNK), BF16),
            jax.ShapeDtypeStruct((b, GDN_HEADS, nc, GDN_DK, CHUNK), BF16),
            jax.ShapeDtypeStruct((b, GDN_HEADS, nc, 1, GDN_DV), F32),
        ],
        compiler_params=_cparams(("parallel",)),
        name="gdn_prep",
    )(proj, proj, proj, conv_w, a_log_pat, dtb_pat, a_log_col, dtb_col)


def _gdnscan_kernel(wq_ref, u_ref, aq_ref, kdt_ref, eg_ref, o_ref, s_scr, *, nb, cb):
    @pl.when(pl.program_id(0) == 0)
    def _():
        s_scr[...] = jnp.zeros(s_scr.shape, F32)

    def chunk(c, carry):
        for bi in range(nb):
            for h in range(GDN_HEADS):
                s = s_scr[bi, h]
                r = _dot(wq_ref[bi, h, c], s.astype(BF16))
                v_new = u_ref[bi, h, c] - r[:CHUNK]
                vb = v_new.astype(BF16)
                o_ref[bi, h, c] = r[CHUNK:] + _dot(aq_ref[bi, h, c], vb)
                s_scr[bi, h] = s * eg_ref[bi, h, c] + _dot(kdt_ref[bi, h, c], vb)
        return carry

    lax.fori_loop(0, cb, chunk, 0)


def _gdn_scan(wq, u, aq, kdt, eg, cb):
    b, _, nc = wq.shape[:3]
    spec = lambda d0, d1: pl.BlockSpec((b, GDN_HEADS, cb, d0, d1), lambda i: (0, 0, i, 0, 0))
    return pl.pallas_call(
        functools.partial(_gdnscan_kernel, nb=b, cb=cb),
        grid=(nc // cb,),
        in_specs=[spec(2 * CHUNK, GDN_DK), spec(CHUNK, GDN_DV), spec(CHUNK, CHUNK),
                  spec(GDN_DK, CHUNK), spec(1, GDN_DV)],
        out_specs=spec(CHUNK, GDN_DV),
        out_shape=jax.ShapeDtypeStruct((b, GDN_HEADS, nc, CHUNK, GDN_DV), F32),
        scratch_shapes=[pltpu.VMEM((b, GDN_HEADS, GDN_DK, GDN_DV), F32)],
        compiler_params=_cparams(("arbitrary",)),
        name="gdn_scan",
    )(wq, u, aq, kdt, eg)


def _merge_kernel(x_ref, oa_ref, z_ref, ob_ref, gate_ref, gn_ref, woa_ref, wob_ref, wout_ref, o_ref):
    parts = []
    for h in range(GDN_HEADS):
        o = oa_ref[0, h]
        ms = jnp.mean(o * o, axis=-1, keepdims=True)
        z = z_ref[:, h * GDN_DV:(h + 1) * GDN_DV]
        parts.append((o * lax.rsqrt(ms + EPS) * gn_ref[...] * _silu(z)).astype(BF16))
    y_a = _dot(jnp.concatenate(parts, axis=1), woa_ref[...])
    y_b = _dot(ob_ref[...], wob_ref[...])
    g = gate_ref[...]
    merged = _sigmoid(g[:, :D_MODEL]) * y_a + _sigmoid(g[:, D_MODEL:]) * y_b
    o_ref[...] = x_ref[...] + _dot(merged.astype(BF16), wout_ref[...])


def _merge(x2d, o_a, proj, o_b, gdn_gain, w_o_a, w_o_b, w_out, t, tm):
    m = x2d.shape[0]
    npb = t // tm
    small = lambda s: pl.BlockSpec(s, lambda i: (0,) * len(s))
    return pl.pallas_call(
        _merge_kernel,
        grid=(m // tm,),
        in_specs=[
            pl.BlockSpec((tm, D_MODEL), lambda i: (i, 0)),
            pl.BlockSpec((1, GDN_HEADS, tm, GDN_DV), lambda i: (i // npb, 0, i % npb, 0)),
            pl.BlockSpec((tm, GDN_V), lambda i: (i, COL_Z // GDN_V)),
            pl.BlockSpec((tm, DIFF_V), lambda i: (i, 0)),
            pl.BlockSpec((tm, 2 * D_MODEL), lambda i: (i, COL_GATE // (2 * D_MODEL))),
            small((1, GDN_DV)), small((GDN_V, D_MODEL)), small((DIFF_V, D_MODEL)), small((D_MODEL, D_MODEL)),
        ],
        out_specs=pl.BlockSpec((tm, D_MODEL), lambda i: (i, 0)),
        out_shape=jax.ShapeDtypeStruct((m, D_MODEL), F32),
        compiler_params=_cparams(("parallel",)),
        name="merge",
    )(x2d, o_a, proj, o_b, proj, gdn_gain, w_o_a, w_o_b, w_out)


def _mlp_kernel(x_ref, mg_ref, wup_ref, wdn_ref, pg_ref, wpg_ref, p_ref, wp_ref, o_ref, h_scr, acc_scr):
    f = pl.program_id(1)

    @pl.when(f == 0)
    def _():
        x = x_ref[...]
        ms = jnp.mean(x * x, axis=-1, keepdims=True)
        h_scr[...] = (x * lax.rsqrt(ms + EPS) * mg_ref[...]).astype(BF16)
        acc_scr[...] = jnp.zeros(acc_scr.shape, F32)

    a = jnp.maximum(_dot(h_scr[...], wup_ref[...]), 0.0)
    acc_scr[...] += _dot((a * a).astype(BF16), wdn_ref[...])

    @pl.when(f == pl.num_programs(1) - 1)
    def _():
        x2 = x_ref[...] + acc_scr[...]
        ms = jnp.mean(x2 * x2, axis=-1, keepdims=True)
        hn = (x2 * lax.rsqrt(ms + EPS) * pg_ref[...]).astype(BF16)
        gate = _sigmoid(_dot(hn, wpg_ref[...]))
        o_ref[...] = x2 + gate * _dot(p_ref[...].astype(BF16), wp_ref[...])


def _mlp_ple(x1, mlp_gain, w_up, w_down, ple_gain, w_ple_gate, p2d, w_ple, tm, tf):
    m = x1.shape[0]
    small = lambda s: pl.BlockSpec(s, lambda i, f: (0,) * len(s))
    return pl.pallas_call(
        _mlp_kernel,
        grid=(m // tm, D_FF // tf),
        in_specs=[
            pl.BlockSpec((tm, D_MODEL), lambda i, f: (i, 0)),
            small((1, D_MODEL)),
            pl.BlockSpec((D_MODEL, tf), lambda i, f: (0, f)),
            pl.BlockSpec((tf, D_MODEL), lambda i, f: (f, 0)),
            small((1, D_MODEL)), small((D_MODEL, D_MODEL)),
            pl.BlockSpec((tm, PLE_DIM), lambda i, f: (i, 0)),
            small((PLE_DIM, D_MODEL)),
        ],
        out_specs=pl.BlockSpec((tm, D_MODEL), lambda i, f: (i, 0)),
        out_shape=jax.ShapeDtypeStruct((m, D_MODEL), F32),
        scratch_shapes=[pltpu.VMEM((tm, D_MODEL), BF16), pltpu.VMEM((tm, D_MODEL), F32)],
        compiler_params=_cparams(("parallel", "arbitrary")),
        name="mlp_ple",
    )(x1, mlp_gain, w_up, w_down, ple_gain, w_ple_gate, p2d, w_ple)


def _reorder_w_in(w):
    o_z = 3 * GDN_QK
    o_a = o_z + GDN_V
    o_dq = o_a + 2 * GDN_HEADS
    o_gate = o_dq + 3 * DIFF_QK
    pad = jnp.zeros((D_MODEL, N_PAD - COL_AB - 2 * GDN_HEADS), w.dtype)
    return jnp.concatenate(
        [w[:, :o_a], w[:, o_gate:], w[:, o_dq:o_gate], w[:, o_a:o_dq], pad], axis=1).astype(BF16)


def _lane_pat(vec, offset):
    return jnp.zeros((1, LANE), F32).at[0, offset:offset + GDN_HEADS].set(vec.astype(F32))


def _col_pat(vec):
    return jnp.zeros((8, 1), F32).at[:GDN_HEADS, 0].set(vec.astype(F32))


def _pick(n, pref):
    return pref if n % pref == 0 else n


def kernel(x, p, positions, attn_norm, w_in, conv_w, a_log, dt_bias, gdn_norm, w_o_a, q_norm, k_norm,
           lambda_q1, lambda_k1, lambda_q2, lambda_k2, diff_norm, w_o_b, w_out, mlp_norm, w_up, w_down,
           ple_norm, w_ple_gate, w_ple):
    b, t, _ = x.shape
    m = b * t
    depth = w_in.shape[0]
    assert t % GDN_BLK == 0, "sequence length must be a multiple of the gdn_prep block"

    tm_big = _pick(m, 1024)
    tm_mid = _pick(m, 512)
    tq = _pick(t, 256)
    tk = _pick(t, 512)
    assert tk % tq == 0
    cb = _pick(t // CHUNK, 4)
    pos3 = positions.reshape(m // tk, 1, tk)
    freq = (ROPE_THETA ** (-jnp.arange(0, ROT_DIM, 2, dtype=F32) / ROT_DIM)).reshape(ROT_DIM // 2, 1)
    row = lambda v: v.astype(F32).reshape(1, -1)

    x2d = x.reshape(m, D_MODEL)
    for i in range(depth):
        lam_init = 0.8 - 0.6 * math.exp(-0.3 * i)
        proj = _in_proj(x2d, row(attn_norm[i]), _reorder_w_in(w_in[i]), tm_big, 1024)

        dqt, dk, dvt = _diff_prep(proj, pos3, freq, jnp.tile(row(q_norm[i]), (1, DIFF_QK // DIFF_DH)),
                                  jnp.tile(row(k_norm[i]), (1, DIFF_QK // DIFF_DH)), tk)
        o_b = _flash_diff(dqt, dk, dvt, row(lambda_q1[i]), row(lambda_k1[i]), row(lambda_q2[i]),
                          row(lambda_k2[i]), row(diff_norm[i]), b, t, tq, tk, lam_init)

        wq, u, aq, kdt, eg = _gdn_prep(proj, conv_w[i].astype(F32), _lane_pat(a_log[i], 0),
                                       _lane_pat(dt_bias[i], 0), _col_pat(a_log[i]), _col_pat(dt_bias[i]), b, t)
        o_a = _gdn_scan(wq, u, aq, kdt, eg, cb).reshape(b, GDN_HEADS, t, GDN_DV)

        x1 = _merge(x2d, o_a, proj, o_b, row(gdn_norm[i]), w_o_a[i].astype(BF16), w_o_b[i].astype(BF16),
                    w_out[i].astype(BF16), t, _pick(t, 512))
        x2d = _mlp_ple(x1, row(mlp_norm[i]), w_up[i].astype(BF16), w_down[i].astype(BF16), row(ple_norm[i]),
                       w_ple_gate[i].astype(BF16), p[i].reshape(m, PLE_DIM), w_ple[i].astype(BF16),
                       tm_mid, 1024)
    return x2d.reshape(b, t, D_MODEL)
```

```python
import functools
import math

import jax
import jax.numpy as jnp
from jax import lax
from jax.experimental import pallas as pl
from jax.experimental.pallas import tpu as pltpu

F32 = jnp.float32
BF16 = jnp.bfloat16

D_MODEL = 1024
PLE_DIM = 256
EPS = 1e-6
GDN_HEADS = 4
GDN_DK = 128
GDN_DV = 128
CONV_K = 4
CHUNK = 64
DIFF_HEADS = 4
DIFF_DH = 64
DIFF_DV = 2 * DIFF_DH
ROT_DIM = DIFF_DH // 4
ROPE_THETA = 500000.0
D_FF = 4 * D_MODEL

GDN_QK = GDN_HEADS * GDN_DK
GDN_V = GDN_HEADS * GDN_DV
DIFF_QK = DIFF_HEADS * 2 * DIFF_DH
DIFF_V = DIFF_HEADS * DIFF_DV
D_IN = 4 * GDN_QK + 2 * GDN_HEADS + 3 * DIFF_QK + 2 * D_MODEL

COL_QKV = 0
COL_Z = 1536
COL_GATE = 2048
COL_DQ = 4096
COL_DK = 4608
COL_DV = 5120
COL_AB = 5632
N_PAD = 6144

LANE = 128
GDN_BLK = 256
CPB = GDN_BLK // CHUNK
FLASH_HPB = 2
LOG2E = math.log2(math.e)
INV_BASE = 16
INV_PASSES = 1
VMEM_LIMIT = 48 * 1024 * 1024

HI = lax.Precision.HIGHEST


def _cparams(sem):
    return pltpu.CompilerParams(dimension_semantics=sem, vmem_limit_bytes=VMEM_LIMIT)


def _dot(a, b):
    return jnp.dot(a, b, preferred_element_type=F32)


def _dot_nt(a, b):
    return lax.dot_general(a, b, (((1,), (1,)), ((), ())), preferred_element_type=F32)


def _split_bf16(a):
    hi = a.astype(BF16)
    lo = (a - hi.astype(F32)).astype(BF16)
    return hi, lo


def _dot3(a, b):
    ah, al = _split_bf16(a)
    bh, bl = _split_bf16(b)
    return _dot(ah, bh) + _dot(ah, bl) + _dot(al, bh)


def _tdot(a, b):
    if INV_PASSES == 3:
        return _dot3(a, b)
    return _dot(a.astype(BF16), b.astype(BF16))


def _sigmoid(x):
    return 1.0 / (1.0 + jnp.exp(-x))


def _silu(x):
    return x * _sigmoid(x)


def _softplus(x):
    return jnp.maximum(x, 0.0) + jnp.log(1.0 + jnp.exp(-jnp.abs(x)))


def _inproj_kernel(x_ref, g_ref, w_ref, o_ref, h_scr):
    @pl.when(pl.program_id(1) == 0)
    def _():
        x = x_ref[...]
        ms = jnp.mean(x * x, axis=-1, keepdims=True)
        h_scr[...] = (x * lax.rsqrt(ms + EPS) * g_ref[...]).astype(BF16)

    o_ref[...] = _dot(h_scr[...], w_ref[...])


def _in_proj(x2d, gain, w_bf16, tm, tn):
    m = x2d.shape[0]
    return pl.pallas_call(
        _inproj_kernel,
        grid=(m // tm, N_PAD // tn),
        in_specs=[
            pl.BlockSpec((tm, D_MODEL), lambda i, j: (i, 0)),
            pl.BlockSpec((1, D_MODEL), lambda i, j: (0, 0)),
            pl.BlockSpec((D_MODEL, tn), lambda i, j: (0, j)),
        ],
        out_specs=pl.BlockSpec((tm, tn), lambda i, j: (i, j)),
        out_shape=jax.ShapeDtypeStruct((m, N_PAD), F32),
        scratch_shapes=[pltpu.VMEM((tm, D_MODEL), BF16)],
        compiler_params=_cparams(("parallel", "arbitrary")),
        name="in_proj",
    )(x2d, gain, w_bf16)


def _diffprep_kernel(pos_ref, freq_ref, q_ref, k_ref, v_ref, qg_ref, kg_ref, oq_ref, ok_ref, ov_ref):
    tm = q_ref.shape[0]
    ang = freq_ref[...] * pos_ref[0].astype(F32)
    cos_t, sin_t = jnp.cos(ang), jnp.sin(ang)
    half = ROT_DIM // 2
    fi = lax.broadcasted_iota(jnp.int32, (half, DIFF_QK), 0)
    li = lax.broadcasted_iota(jnp.int32, (half, DIFF_QK), 1)
    d = jnp.bitwise_and(li, DIFF_DH - 1)
    e_lo = jnp.where(d == fi, 1.0, 0.0).astype(F32)
    e_hi = jnp.where(d == fi + half, 1.0, 0.0).astype(F32)
    tdot = lambda a, b: lax.dot_general(a, b, (((0,), (0,)), ((), ())), precision=HI,
                                        preferred_element_type=F32)
    lane = lax.broadcasted_iota(jnp.int32, (1, DIFF_QK), 1)
    passthrough = jnp.where(jnp.bitwise_and(lane, DIFF_DH - 1) >= ROT_DIM, 1.0, 0.0).astype(F32)
    cos_f = tdot(cos_t, e_lo + e_hi) + passthrough
    sin_a = -tdot(sin_t, e_lo)
    sin_b = tdot(sin_t, e_hi)

    ri = lax.broadcasted_iota(jnp.int32, (DIFF_QK, DIFF_QK), 0)
    ci = lax.broadcasted_iota(jnp.int32, (DIFF_QK, DIFF_QK), 1)
    grp = jnp.where(jnp.right_shift(ri, 6) == jnp.right_shift(ci, 6), 1.0 / DIFF_DH, 0.0).astype(BF16)

    def norm_rot(x, gain, scale):
        sq = x * x
        hi, lo = _split_bf16(sq)
        ms = _dot(hi, grp) + _dot(lo, grp)
        y = x * lax.rsqrt(ms + EPS) * gain
        up = pltpu.roll(y, DIFF_QK - half, 1)
        dn = pltpu.roll(y, half, 1)
        return (y * cos_f + up * sin_a + dn * sin_b) * scale

    oq_ref[0] = norm_rot(q_ref[...], qg_ref[...], DIFF_DH ** -0.5 * LOG2E).T.astype(BF16)
    ok_ref[...] = norm_rot(k_ref[...], kg_ref[...], 1.0).astype(BF16)
    ov_ref[0] = v_ref[...].T.astype(BF16)


def _diff_prep(proj, pos3, freq, qg, kg, tm):
    m = proj.shape[0]
    blk = lambda c: pl.BlockSpec((tm, DIFF_QK), lambda i: (i, c))
    small = lambda s: pl.BlockSpec(s, lambda i: (0,) * len(s))
    out_t = jax.ShapeDtypeStruct((m // tm, DIFF_QK, tm), BF16)
    spec_t = pl.BlockSpec((1, DIFF_QK, tm), lambda i: (i, 0, 0))
    return pl.pallas_call(
        _diffprep_kernel,
        grid=(m // tm,),
        in_specs=[
            pl.BlockSpec((1, 1, tm), lambda i: (i, 0, 0)),
            small((ROT_DIM // 2, 1)),
            blk(COL_DQ // DIFF_QK), blk(COL_DK // DIFF_QK), blk(COL_DV // DIFF_QK),
            small((1, DIFF_QK)), small((1, DIFF_QK)),
        ],
        out_specs=[spec_t, pl.BlockSpec((tm, DIFF_QK), lambda i: (i, 0)), spec_t],
        out_shape=[out_t, jax.ShapeDtypeStruct((m, DIFF_QK), BF16), out_t],
        compiler_params=_cparams(("parallel",)),
        name="diff_prep",
    )(pos3, freq, proj, proj, proj, qg, kg)


def _flash_kernel(qt_ref, k_ref, vt_ref, lq1_ref, lk1_ref, lq2_ref, lk2_ref, gain_ref, o_ref,
                  acc_scr, *, tq, tk, lam_init):
    qi = pl.program_id(2)
    chains = [(hh, c) for hh in range(FLASH_HPB) for c in range(2)]
    qst = {}
    for hh in range(FLASH_HPB):
        qt = qt_ref[0, hh * LANE:(hh + 1) * LANE, :]
        row = lax.broadcasted_iota(jnp.int32, qt.shape, 0)
        zero = jnp.zeros_like(qt)
        qst[hh, 0] = jnp.where(row < DIFF_DH, qt, zero)
        qst[hh, 1] = jnp.where(row >= DIFF_DH, qt, zero)

    acc_scr[...] = jnp.zeros(acc_scr.shape, F32)

    def step(j, carry, masked):
        start = pl.multiple_of(j * tk, tk)
        if masked:
            r = lax.broadcasted_iota(jnp.int32, (tk, tq), 0)
            c = lax.broadcasted_iota(jnp.int32, (tk, tq), 1)
            valid = start + r <= qi * tq + c
        scores = [_dot(k_ref[pl.ds(start, tk), hh * LANE:(hh + 1) * LANE], qst[hh, comp])
                  for hh, comp in chains]
        out, probs, alphas = [], [], []
        for idx, s in enumerate(scores):
            m_old, l_old = carry[idx]
            if masked:
                s = jnp.where(valid, s, -jnp.inf)
            m_new = jnp.maximum(m_old, jnp.max(s, axis=0, keepdims=True))
            alpha = jnp.exp2(m_old - m_new)
            p = jnp.exp2(s - m_new)
            out.append((m_new, alpha * l_old + jnp.sum(p, axis=0, keepdims=True)))
            probs.append(p.astype(BF16))
            alphas.append(alpha)
        for idx, (hh, comp) in enumerate(chains):
            pv = _dot(vt_ref[j, hh * LANE:(hh + 1) * LANE, :], probs[idx])
            acc_scr[hh, comp] = alphas[idx] * acc_scr[hh, comp] + pv
        return tuple(out)

    n_full = (qi * tq) // tk
    init = tuple((jnp.full((1, tq), -jnp.inf, F32), jnp.zeros((1, tq), F32)) for _ in chains)
    carry = lax.fori_loop(0, n_full, lambda j, cr: step(j, cr, False), init)
    carry = step(n_full, carry, True)

    lam = (jnp.exp(jnp.sum(lq1_ref[...] * lk1_ref[...], axis=-1, keepdims=True))
           - jnp.exp(jnp.sum(lq2_ref[...] * lk2_ref[...], axis=-1, keepdims=True)) + lam_init)
    for hh in range(FLASH_HPB):
        o1 = acc_scr[hh, 0] / carry[2 * hh][1]
        o2 = acc_scr[hh, 1] / carry[2 * hh + 1][1]
        o = (o1 - lam * o2).T
        ms = jnp.mean(o * o, axis=-1, keepdims=True)
        o_ref[:, hh * LANE:(hh + 1) * LANE] = (
            o * lax.rsqrt(ms + EPS) * gain_ref[...] * (1.0 - lam_init)).astype(o_ref.dtype)


def _flash_diff(dqt, dk, dvt, lq1, lk1, lq2, lk2, gain, b, t, tq, tk, lam_init):
    nq = t // tq
    nkb = t // tk
    qpb = tk // tq
    hw = FLASH_HPB * LANE
    small = lambda s: pl.BlockSpec(s, lambda bi, h, qi: (0,) * len(s))
    return pl.pallas_call(
        functools.partial(_flash_kernel, tq=tq, tk=tk, lam_init=lam_init),
        grid=(b, DIFF_HEADS // FLASH_HPB, nq),
        in_specs=[
            pl.BlockSpec((1, hw, tq), lambda bi, h, qi: (bi * nkb + qi // qpb, h, qi % qpb)),
            pl.BlockSpec((t, hw), lambda bi, h, qi: (bi, h)),
            pl.BlockSpec((nkb, hw, tk), lambda bi, h, qi: (bi, h, 0)),
            small((1, DIFF_DH)), small((1, DIFF_DH)), small((1, DIFF_DH)), small((1, DIFF_DH)),
            small((1, DIFF_DV)),
        ],
        out_specs=pl.BlockSpec((tq, hw), lambda bi, h, qi: (bi * nq + qi, h)),
        out_shape=jax.ShapeDtypeStruct((b * t, DIFF_V), BF16),
        scratch_shapes=[pltpu.VMEM((FLASH_HPB, 2, DIFF_DV, tq), F32)],
        compiler_params=_cparams(("parallel", "parallel", "arbitrary")),
        name="flash_diff",
    )(dqt, dk, dvt, lq1, lk1, lq2, lk2, gain)


def _gdnprep_kernel(cur_ref, halo_ref, ab_ref, convw_ref, alog_ref, dtb_ref, alogc_ref, dtbc_ref,
                    wq_ref, u_ref, aq_ref, kdt_ref, eg_ref, *, nblk):
    i = pl.program_id(0)
    n = GDN_BLK
    keep = jnp.where(i % nblk == 0, 0.0, 1.0).astype(F32)
    xs = jnp.concatenate([halo_ref[...] * keep, cur_ref[...]], axis=0)
    cw = convw_ref[...]
    conv = xs[8:8 + n] * cw[CONV_K - 1:CONV_K]
    for j in range(CONV_K - 1):
        off = 8 - (CONV_K - 1) + j
        conv = conv + xs[off:off + n] * cw[j:j + 1]
    qkv = _silu(conv)

    ab = ab_ref[...]
    ab_t = ab.T
    g_col = -jnp.exp(alog_ref[...]) * _softplus(ab + dtb_ref[...])
    beta_col = _sigmoid(ab)
    g_row = -jnp.exp(alogc_ref[...]) * _softplus(ab_t[:8] + dtbc_ref[...])

    ri = lax.broadcasted_iota(jnp.int32, (n, n), 0)
    ci = lax.broadcasted_iota(jnp.int32, (n, n), 1)
    same = jnp.right_shift(ri, 6) == jnp.right_shift(ci, 6)
    same32 = jnp.right_shift(ri, 5) == jnp.right_shift(ci, 5)
    same16 = jnp.right_shift(ri, 4) == jnp.right_shift(ci, 4)
    incl = jnp.logical_and(same, ri >= ci)
    strict = jnp.logical_and(same, ri > ci)
    one = jnp.ones((n, n), F32)
    zero = jnp.zeros((n, n), F32)
    tri = jnp.where(incl, one, zero)
    blk_ones = jnp.where(same, one, zero)
    eye = jnp.where(ri == ci, one, zero)
    hdot = lambda a, b: jnp.dot(a, b, precision=HI, preferred_element_type=F32)
    gc_col = hdot(tri, g_col)
    gl_col = hdot(blk_ones, g_col)
    gc_row = lax.dot_general(g_row, tri, (((1,), (1,)), ((), ())), precision=HI,
                             preferred_element_type=F32)

    heads = range(GDN_HEADS)
    qs, ks, kbs, kbfs, decays, egcs, glbs, gcbs, rhss = [], [], [], [], [], [], [], [], []
    for h in heads:
        q = qkv[:, h * GDN_DK:(h + 1) * GDN_DK]
        k = qkv[:, GDN_QK + h * GDN_DK: GDN_QK + (h + 1) * GDN_DK]
        v = qkv[:, 2 * GDN_QK + h * GDN_DV: 2 * GDN_QK + (h + 1) * GDN_DV]
        q = q * lax.rsqrt(jnp.sum(q * q, axis=-1, keepdims=True) + EPS) * (GDN_DK ** -0.5)
        k = k * lax.rsqrt(jnp.sum(k * k, axis=-1, keepdims=True) + EPS)
        beta = beta_col[:, GDN_HEADS + h: GDN_HEADS + h + 1]
        gcb = jnp.broadcast_to(gc_col[:, h:h + 1], (n, GDN_DK))
        glb = jnp.broadcast_to(gl_col[:, h:h + 1], (n, GDN_DK))
        gdiff = jnp.concatenate([gcb, gcb], axis=1) - gc_row[h:h + 1, :]
        kb = k * beta
        egc = jnp.exp(gcb)
        qs.append(q); ks.append(k); kbs.append(kb); kbfs.append(k.astype(BF16))
        decays.append(jnp.where(incl, jnp.exp(jnp.where(incl, gdiff, zero)), zero))
        egcs.append(egc); glbs.append(glb); gcbs.append(gcb)
        rhss.append(jnp.concatenate([v * beta, kb * egc], axis=1).astype(BF16))

    kk = [_dot_nt(kbs[h].astype(BF16), kbfs[h]) for h in heads]
    qk = [_dot_nt(qs[h].astype(BF16), kbfs[h]) for h in heads]
    lowers = [jnp.where(strict, kk[h] * decays[h], zero) for h in heads]
    xs_ = [jnp.where(same16, -lowers[h], zero) for h in heads]
    tmats = [eye + xs_[h] for h in heads]
    for _ in range(int(math.log2(INV_BASE)) - 1):
        xs_ = [_tdot(xs_[h], xs_[h]) for h in heads]
        tmats = [tmats[h] + _tdot(tmats[h], xs_[h]) for h in heads]
    for lo_mask, hi_mask in ((same16, same32), (same32, same)):
        sel = jnp.logical_and(hi_mask, jnp.logical_not(lo_mask))
        tmp = [_tdot(tmats[h], jnp.where(sel, lowers[h], zero)) for h in heads]
        tmats = [tmats[h] - _tdot(tmp[h], tmats[h]) for h in heads]
    uws = [_dot(tmats[h].astype(BF16), rhss[h]) for h in heads]

    for h in heads:
        u = uws[h][:, :GDN_DV]
        w = uws[h][:, GDN_DV:].astype(BF16)
        a_intra = (qk[h] * decays[h]).astype(BF16)
        q_dec = (qs[h] * egcs[h]).astype(BF16)
        kd_t = (ks[h] * jnp.exp(glbs[h] - gcbs[h])).T.astype(BF16)
        eg = jnp.exp(glbs[h])
        for c in range(CPB):
            rs = slice(c * CHUNK, (c + 1) * CHUNK)
            wq_ref[0, h, c, 0:CHUNK, :] = w[rs]
            wq_ref[0, h, c, CHUNK:2 * CHUNK, :] = q_dec[rs]
            u_ref[0, h, c] = u[rs]
            aq_ref[0, h, c] = a_intra[rs, rs]
            kdt_ref[0, h, c] = kd_t[:, rs]
            eg_ref[0, h, c] = eg[c * CHUNK: c * CHUNK + 1, :]


def _gdn_prep(proj, conv_w, a_log_pat, dtb_pat, a_log_col, dtb_col, b, t):
    nblk = t // GDN_BLK
    nc = t // CHUNK
    c3 = 2 * GDN_QK + GDN_V
    small = lambda s: pl.BlockSpec(s, lambda i: (0,) * len(s))
    omap = lambda i: (i // nblk, 0, i % nblk, 0, 0)
    return pl.pallas_call(
        functools.partial(_gdnprep_kernel, nblk=nblk),
        grid=(b * nblk,),
        in_specs=[
            pl.BlockSpec((GDN_BLK, c3), lambda i: (i, COL_QKV // c3)),
            pl.BlockSpec((8, c3), lambda i: (jnp.maximum(i * (GDN_BLK // 8) - 1, 0), COL_QKV // c3)),
            pl.BlockSpec((GDN_BLK, LANE), lambda i: (i, COL_AB // LANE)),
            small((CONV_K, c3)), small((1, LANE)), small((1, LANE)), small((8, 1)), small((8, 1)),
        ],
        out_specs=[
            pl.BlockSpec((1, GDN_HEADS, CPB, 2 * CHUNK, GDN_DK), omap),
            pl.BlockSpec((1, GDN_HEADS, CPB, CHUNK, GDN_DV), omap),
            pl.BlockSpec((1, GDN_HEADS, CPB, CHUNK, CHUNK), omap),
            pl.BlockSpec((1, GDN_HEADS, CPB, GDN_DK, CHUNK), omap),
            pl.BlockSpec((1, GDN_HEADS, CPB, 1, GDN_DV), omap),
        ],
        out_shape=[
            jax.ShapeDtypeStruct((b, GDN_HEADS, nc, 2 * CHUNK, GDN_DK), BF16),
            jax.ShapeDtypeStruct((b, GDN_HEADS, nc, CHUNK, GDN_DV), F32),
            jax.ShapeDtypeStruct((b, GDN_HEADS, nc, CHUNK, CHUNK), BF16),
            jax.ShapeDtypeStruct((b, GDN_HEADS, nc, GDN_DK, CHUNK), BF16),
            jax.ShapeDtypeStruct((b, GDN_HEADS, nc, 1, GDN_DV), F32),
        ],
        compiler_params=_cparams(("parallel",)),
        name="gdn_prep",
    )(proj, proj, proj, conv_w, a_log_pat, dtb_pat, a_log_col, dtb_col)


def _gdnscan_kernel(wq_ref, u_ref, aq_ref, kdt_ref, eg_ref, o_ref, s_scr, *, nb, cb):
    @pl.when(pl.program_id(0) == 0)
    def _():
        s_scr[...] = jnp.zeros(s_scr.shape, F32)

    streams = [(bi, h) for bi in range(nb) for h in range(GDN_HEADS)]

    def chunk(c, carry):
        rs = [_dot(wq_ref[bi, h, c], s_scr[bi, h].astype(BF16)) for bi, h in streams]
        vbs = [(u_ref[bi, h, c] - r[:CHUNK]).astype(BF16) for (bi, h), r in zip(streams, rs)]
        for (bi, h), r, vb in zip(streams, rs, vbs):
            o_ref[bi, h, c] = r[CHUNK:] + _dot(aq_ref[bi, h, c], vb)
        for (bi, h), vb in zip(streams, vbs):
            s_scr[bi, h] = s_scr[bi, h] * eg_ref[bi, h, c] + _dot(kdt_ref[bi, h, c], vb)
        return carry

    lax.fori_loop(0, cb, chunk, 0)


def _gdn_scan(wq, u, aq, kdt, eg, cb):
    b, _, nc = wq.shape[:3]
    spec = lambda d0, d1: pl.BlockSpec((b, GDN_HEADS, cb, d0, d1), lambda i: (0, 0, i, 0, 0))
    return pl.pallas_call(
        functools.partial(_gdnscan_kernel, nb=b, cb=cb),
        grid=(nc // cb,),
        in_specs=[spec(2 * CHUNK, GDN_DK), spec(CHUNK, GDN_DV), spec(CHUNK, CHUNK),
                  spec(GDN_DK, CHUNK), spec(1, GDN_DV)],
        out_specs=spec(CHUNK, GDN_DV),
        out_shape=jax.ShapeDtypeStruct((b, GDN_HEADS, nc, CHUNK, GDN_DV), F32),
        scratch_shapes=[pltpu.VMEM((b, GDN_HEADS, GDN_DK, GDN_DV), F32)],
        compiler_params=_cparams(("arbitrary",)),
        name="gdn_scan",
    )(wq, u, aq, kdt, eg)


def _merge_kernel(x_ref, oa_ref, z_ref, ob_ref, gate_ref, gn_ref, woa_ref, wob_ref, wout_ref, o_ref):
    parts = []
    for h in range(GDN_HEADS):
        o = oa_ref[0, h]
        ms = jnp.mean(o * o, axis=-1, keepdims=True)
        z = z_ref[:, h * GDN_DV:(h + 1) * GDN_DV]
        parts.append((o * lax.rsqrt(ms + EPS) * gn_ref[...] * _silu(z)).astype(BF16))
    y_a = _dot(jnp.concatenate(parts, axis=1), woa_ref[...])
    y_b = _dot(ob_ref[...], wob_ref[...])
    g = gate_ref[...]
    merged = _sigmoid(g[:, :D_MODEL]) * y_a + _sigmoid(g[:, D_MODEL:]) * y_b
    o_ref[...] = x_ref[...] + _dot(merged.astype(BF16), wout_ref[...])


def _merge(x2d, o_a, proj, o_b, gdn_gain, w_o_a, w_o_b, w_out, t, tm):
    m = x2d.shape[0]
    npb = t // tm
    small = lambda s: pl.BlockSpec(s, lambda i: (0,) * len(s))
    return pl.pallas_call(
        _merge_kernel,
        grid=(m // tm,),
        in_specs=[
            pl.BlockSpec((tm, D_MODEL), lambda i: (i, 0)),
            pl.BlockSpec((1, GDN_HEADS, tm, GDN_DV), lambda i: (i // npb, 0, i % npb, 0)),
            pl.BlockSpec((tm, GDN_V), lambda i: (i, COL_Z // GDN_V)),
            pl.BlockSpec((tm, DIFF_V), lambda i: (i, 0)),
            pl.BlockSpec((tm, 2 * D_MODEL), lambda i: (i, COL_GATE // (2 * D_MODEL))),
            small((1, GDN_DV)), small((GDN_V, D_MODEL)), small((DIFF_V, D_MODEL)), small((D_MODEL, D_MODEL)),
        ],
        out_specs=pl.BlockSpec((tm, D_MODEL), lambda i: (i, 0)),
        out_shape=jax.ShapeDtypeStruct((m, D_MODEL), F32),
        compiler_params=_cparams(("parallel",)),
        name="merge",
    )(x2d, o_a, proj, o_b, proj, gdn_gain, w_o_a, w_o_b, w_out)


def _mlp_kernel(x_ref, mg_ref, wup_ref, wdn_ref, pg_ref, wpg_ref, p_ref, wp_ref, o_ref, h_scr, acc_scr):
    f = pl.program_id(1)

    @pl.when(f == 0)
    def _():
        x = x_ref[...]
        ms = jnp.mean(x * x, axis=-1, keepdims=True)
        h_scr[...] = (x * lax.rsqrt(ms + EPS) * mg_ref[...]).astype(BF16)
        acc_scr[...] = jnp.zeros(acc_scr.shape, F32)

    a = jnp.maximum(_dot(h_scr[...], wup_ref[...]), 0.0)
    acc_scr[...] += _dot((a * a).astype(BF16), wdn_ref[...])

    @pl.when(f == pl.num_programs(1) - 1)
    def _():
        x2 = x_ref[...] + acc_scr[...]
        ms = jnp.mean(x2 * x2, axis=-1, keepdims=True)
        hn = (x2 * lax.rsqrt(ms + EPS) * pg_ref[...]).astype(BF16)
        gate = _sigmoid(_dot(hn, wpg_ref[...]))
        o_ref[...] = x2 + gate * _dot(p_ref[...].astype(BF16), wp_ref[...])


def _mlp_ple(x1, mlp_gain, w_up, w_down, ple_gain, w_ple_gate, p2d, w_ple, tm, tf):
    m = x1.shape[0]
    small = lambda s: pl.BlockSpec(s, lambda i, f: (0,) * len(s))
    return pl.pallas_call(
        _mlp_kernel,
        grid=(m // tm, D_FF // tf),
        in_specs=[
            pl.BlockSpec((tm, D_MODEL), lambda i, f: (i, 0)),
            small((1, D_MODEL)),
            pl.BlockSpec((D_MODEL, tf), lambda i, f: (0, f)),
            pl.BlockSpec((tf, D_MODEL), lambda i, f: (f, 0)),
            small((1, D_MODEL)), small((D_MODEL, D_MODEL)),
            pl.BlockSpec((tm, PLE_DIM), lambda i, f: (i, 0)),
            small((PLE_DIM, D_MODEL)),
        ],
        out_specs=pl.BlockSpec((tm, D_MODEL), lambda i, f: (i, 0)),
        out_shape=jax.ShapeDtypeStruct((m, D_MODEL), F32),
        scratch_shapes=[pltpu.VMEM((tm, D_MODEL), BF16), pltpu.VMEM((tm, D_MODEL), F32)],
        compiler_params=_cparams(("parallel", "arbitrary")),
        name="mlp_ple",
    )(x1, mlp_gain, w_up, w_down, ple_gain, w_ple_gate, p2d, w_ple)


def _reorder_w_in(w):
    o_z = 3 * GDN_QK
    o_a = o_z + GDN_V
    o_dq = o_a + 2 * GDN_HEADS
    o_gate = o_dq + 3 * DIFF_QK
    pad = jnp.zeros((D_MODEL, N_PAD - COL_AB - 2 * GDN_HEADS), w.dtype)
    return jnp.concatenate(
        [w[:, :o_a], w[:, o_gate:], w[:, o_dq:o_gate], w[:, o_a:o_dq], pad], axis=1).astype(BF16)


def _lane_pat(vec, offset):
    return jnp.zeros((1, LANE), F32).at[0, offset:offset + GDN_HEADS].set(vec.astype(F32))


def _col_pat(vec):
    return jnp.zeros((8, 1), F32).at[:GDN_HEADS, 0].set(vec.astype(F32))


def _pick(n, pref):
    return pref if n % pref == 0 else n


def kernel(x, p, positions, attn_norm, w_in, conv_w, a_log, dt_bias, gdn_norm, w_o_a, q_norm, k_norm,
           lambda_q1, lambda_k1, lambda_q2, lambda_k2, diff_norm, w_o_b, w_out, mlp_norm, w_up, w_down,
           ple_norm, w_ple_gate, w_ple):
    b, t, _ = x.shape
    m = b * t
    depth = w_in.shape[0]
    assert t % GDN_BLK == 0, "sequence length must be a multiple of the gdn_prep block"

    tm_big = _pick(m, 1024)
    tm_mid = _pick(m, 512)
    tq = _pick(t, 256)
    tk = _pick(t, 512)
    assert tk % tq == 0
    cb = _pick(t // CHUNK, 4)
    pos3 = positions.reshape(m // tk, 1, tk)
    freq = (ROPE_THETA ** (-jnp.arange(0, ROT_DIM, 2, dtype=F32) / ROT_DIM)).reshape(ROT_DIM // 2, 1)
    row = lambda v: v.astype(F32).reshape(1, -1)

    x2d = x.reshape(m, D_MODEL)
    for i in range(depth):
        lam_init = 0.8 - 0.6 * math.exp(-0.3 * i)
        proj = _in_proj(x2d, row(attn_norm[i]), _reorder_w_in(w_in[i]), tm_big, 1024)

        dqt, dk, dvt = _diff_prep(proj, pos3, freq, jnp.tile(row(q_norm[i]), (1, DIFF_QK // DIFF_DH)),
                                  jnp.tile(row(k_norm[i]), (1, DIFF_QK // DIFF_DH)), tk)
        o_b = _flash_diff(dqt, dk, dvt, row(lambda_q1[i]), row(lambda_k1[i]), row(lambda_q2[i]),
                          row(lambda_k2[i]), row(diff_norm[i]), b, t, tq, tk, lam_init)

        wq, u, aq, kdt, eg = _gdn_prep(proj, conv_w[i].astype(F32), _lane_pat(a_log[i], 0),
                                       _lane_pat(dt_bias[i], 0), _col_pat(a_log[i]), _col_pat(dt_bias[i]), b, t)
        o_a = _gdn_scan(wq, u, aq, kdt, eg, cb).reshape(b, GDN_HEADS, t, GDN_DV)

        x1 = _merge(x2d, o_a, proj, o_b, row(gdn_norm[i]), w_o_a[i].astype(BF16), w_o_b[i].astype(BF16),
                    w_out[i].astype(BF16), t, _pick(t, 512))
        x2d = _mlp_ple(x1, row(mlp_norm[i]), w_up[i].astype(BF16), w_down[i].astype(BF16), row(ple_norm[i]),
                       w_ple_gate[i].astype(BF16), p[i].reshape(m, PLE_DIM), w_ple[i].astype(BF16),
                       tm_mid, 1024)
    return x2d.reshape(b, t, D_MODEL)
```

```python
import functools
import math

import jax
import jax.numpy as jnp
from jax import lax
from jax.experimental import pallas as pl
from jax.experimental.pallas import tpu as pltpu

F32 = jnp.float32
BF16 = jnp.bfloat16

D_MODEL = 1024
PLE_DIM = 256
EPS = 1e-6
GDN_HEADS = 4
GDN_DK = 128
GDN_DV = 128
CONV_K = 4
CHUNK = 64
DIFF_HEADS = 4
DIFF_DH = 64
DIFF_DV = 2 * DIFF_DH
ROT_DIM = DIFF_DH // 4
ROPE_THETA = 500000.0
D_FF = 4 * D_MODEL

GDN_QK = GDN_HEADS * GDN_DK
GDN_V = GDN_HEADS * GDN_DV
DIFF_QK = DIFF_HEADS * 2 * DIFF_DH
DIFF_V = DIFF_HEADS * DIFF_DV
D_IN = 4 * GDN_QK + 2 * GDN_HEADS + 3 * DIFF_QK + 2 * D_MODEL

COL_QKV = 0
COL_Z = 1536
COL_GATE = 2048
COL_DQ = 4096
COL_DK = 4608
COL_DV = 5120
COL_AB = 5632
N_PAD = 6144

LANE = 128
GDN_BLK = 256
CPB = GDN_BLK // CHUNK
FLASH_HPB = 2
LOG2E = math.log2(math.e)
INV_BASE = 16
INV_PASSES = 1
VMEM_LIMIT = 48 * 1024 * 1024

HI = lax.Precision.HIGHEST


def _cparams(sem):
    return pltpu.CompilerParams(dimension_semantics=sem, vmem_limit_bytes=VMEM_LIMIT)


def _dot(a, b):
    return jnp.dot(a, b, preferred_element_type=F32)


def _dot_nt(a, b):
    return lax.dot_general(a, b, (((1,), (1,)), ((), ())), preferred_element_type=F32)


def _split_bf16(a):
    hi = a.astype(BF16)
    lo = (a - hi.astype(F32)).astype(BF16)
    return hi, lo


def _dot3(a, b):
    ah, al = _split_bf16(a)
    bh, bl = _split_bf16(b)
    return _dot(ah, bh) + _dot(ah, bl) + _dot(al, bh)


def _tdot(a, b):
    if INV_PASSES == 3:
        return _dot3(a, b)
    return _dot(a.astype(BF16), b.astype(BF16))


def _sigmoid(x):
    return 1.0 / (1.0 + jnp.exp(-x))


def _silu(x):
    return x * _sigmoid(x)


def _softplus(x):
    return jnp.maximum(x, 0.0) + jnp.log(1.0 + jnp.exp(-jnp.abs(x)))


def _inproj_kernel(x_ref, g_ref, w_ref, o_ref, h_scr):
    @pl.when(pl.program_id(1) == 0)
    def _():
        x = x_ref[...]
        ms = jnp.mean(x * x, axis=-1, keepdims=True)
        h_scr[...] = (x * lax.rsqrt(ms + EPS) * g_ref[...]).astype(BF16)

    o_ref[...] = _dot(h_scr[...], w_ref[...])


def _in_proj(x2d, gain, w_bf16, tm, tn):
    m = x2d.shape[0]
    return pl.pallas_call(
        _inproj_kernel,
        grid=(m // tm, N_PAD // tn),
        in_specs=[
            pl.BlockSpec((tm, D_MODEL), lambda i, j: (i, 0)),
            pl.BlockSpec((1, D_MODEL), lambda i, j: (0, 0)),
            pl.BlockSpec((D_MODEL, tn), lambda i, j: (0, j)),
        ],
        out_specs=pl.BlockSpec((tm, tn), lambda i, j: (i, j)),
        out_shape=jax.ShapeDtypeStruct((m, N_PAD), F32),
        scratch_shapes=[pltpu.VMEM((tm, D_MODEL), BF16)],
        compiler_params=_cparams(("parallel", "arbitrary")),
        name="in_proj",
    )(x2d, gain, w_bf16)


def _diffprep_kernel(pos_ref, freq_ref, q_ref, k_ref, v_ref, qg_ref, kg_ref, oq_ref, ok_ref, ov_ref):
    tm = q_ref.shape[0]
    ang = freq_ref[...] * pos_ref[0].astype(F32)
    cos_t, sin_t = jnp.cos(ang), jnp.sin(ang)
    half = ROT_DIM // 2
    fi = lax.broadcasted_iota(jnp.int32, (half, DIFF_QK), 0)
    li = lax.broadcasted_iota(jnp.int32, (half, DIFF_QK), 1)
    d = jnp.bitwise_and(li, DIFF_DH - 1)
    e_lo = jnp.where(d == fi, 1.0, 0.0).astype(F32)
    e_hi = jnp.where(d == fi + half, 1.0, 0.0).astype(F32)
    tdot = lambda a, b: lax.dot_general(a, b, (((0,), (0,)), ((), ())), precision=HI,
                                        preferred_element_type=F32)
    lane = lax.broadcasted_iota(jnp.int32, (1, DIFF_QK), 1)
    passthrough = jnp.where(jnp.bitwise_and(lane, DIFF_DH - 1) >= ROT_DIM, 1.0, 0.0).astype(F32)
    cos_f = tdot(cos_t, e_lo + e_hi) + passthrough
    sin_a = -tdot(sin_t, e_lo)
    sin_b = tdot(sin_t, e_hi)

    ri = lax.broadcasted_iota(jnp.int32, (DIFF_QK, DIFF_QK), 0)
    ci = lax.broadcasted_iota(jnp.int32, (DIFF_QK, DIFF_QK), 1)
    grp = jnp.where(jnp.right_shift(ri, 6) == jnp.right_shift(ci, 6), 1.0 / DIFF_DH, 0.0).astype(BF16)

    def norm_rot(x, gain, scale):
        sq = x * x
        hi, lo = _split_bf16(sq)
        ms = _dot(hi, grp) + _dot(lo, grp)
        y = x * lax.rsqrt(ms + EPS) * gain
        up = pltpu.roll(y, DIFF_QK - half, 1)
        dn = pltpu.roll(y, half, 1)
        return (y * cos_f + up * sin_a + dn * sin_b) * scale

    oq_ref[0] = norm_rot(q_ref[...], qg_ref[...], DIFF_DH ** -0.5 * LOG2E).T.astype(BF16)
    ok_ref[...] = norm_rot(k_ref[...], kg_ref[...], 1.0).astype(BF16)
    ov_ref[0] = v_ref[...].T.astype(BF16)


def _diff_prep(proj, pos3, freq, qg, kg, tm):
    m = proj.shape[0]
    blk = lambda c: pl.BlockSpec((tm, DIFF_QK), lambda i: (i, c))
    small = lambda s: pl.BlockSpec(s, lambda i: (0,) * len(s))
    out_t = jax.ShapeDtypeStruct((m // tm, DIFF_QK, tm), BF16)
    spec_t = pl.BlockSpec((1, DIFF_QK, tm), lambda i: (i, 0, 0))
    return pl.pallas_call(
        _diffprep_kernel,
        grid=(m // tm,),
        in_specs=[
            pl.BlockSpec((1, 1, tm), lambda i: (i, 0, 0)),
            small((ROT_DIM // 2, 1)),
            blk(COL_DQ // DIFF_QK), blk(COL_DK // DIFF_QK), blk(COL_DV // DIFF_QK),
            small((1, DIFF_QK)), small((1, DIFF_QK)),
        ],
        out_specs=[spec_t, pl.BlockSpec((tm, DIFF_QK), lambda i: (i, 0)), spec_t],
        out_shape=[out_t, jax.ShapeDtypeStruct((m, DIFF_QK), BF16), out_t],
        compiler_params=_cparams(("parallel",)),
        name="diff_prep",
    )(pos3, freq, proj, proj, proj, qg, kg)


def _flash_kernel(qt_ref, k_ref, vt_ref, lq1_ref, lk1_ref, lq2_ref, lk2_ref, gain_ref, o_ref,
                  *scratch, tq, tk, lam_init):
    qi = pl.program_id(2)
    chains = [(hh, c) for hh in range(FLASH_HPB) for c in range(2)]
    acc_scr, s_scr, p_scr = (scratch[k * len(chains):(k + 1) * len(chains)] for k in range(3))
    qst = {}
    for hh in range(FLASH_HPB):
        qt = qt_ref[0, hh * LANE:(hh + 1) * LANE, :]
        row = lax.broadcasted_iota(jnp.int32, qt.shape, 0)
        zero = jnp.zeros_like(qt)
        qst[hh, 0] = jnp.where(row < DIFF_DH, qt, zero)
        qst[hh, 1] = jnp.where(row >= DIFF_DH, qt, zero)

    nch = len(chains)
    last = nch - 1
    for c in range(nch):
        acc_scr[c][...] = jnp.zeros((DIFF_DV, tq), F32)

    def qk(c, j):
        hh, comp = chains[c]
        start = pl.multiple_of(j * tk, tk)
        s = _dot(k_ref[pl.ds(start, tk), hh * LANE:(hh + 1) * LANE], qst[hh, comp])
        s_scr[c][...] = s
        return jnp.max(s, axis=0, keepdims=True)

    def pv(c, j, alpha):
        hh, comp = chains[c]
        upd = _dot(vt_ref[j, hh * LANE:(hh + 1) * LANE, :], p_scr[c][...])
        acc_scr[c][...] = alpha * acc_scr[c][...] + upd

    def softmax(c, j, stat, cmax, masked):
        m_old, l_old = stat
        s = s_scr[c][...]
        if masked:
            r = lax.broadcasted_iota(jnp.int32, (tk, tq), 0)
            col = lax.broadcasted_iota(jnp.int32, (tk, tq), 1)
            s = jnp.where(j * tk + r <= qi * tq + col, s, -jnp.inf)
            cmax = jnp.max(s, axis=0, keepdims=True)
        m_new = jnp.maximum(m_old, cmax)
        alpha = jnp.exp2(m_old - m_new)
        p = jnp.exp2(s - m_new)
        p_scr[c][...] = p.astype(BF16)
        return (m_new, alpha * l_old + jnp.sum(p, axis=0, keepdims=True)), alpha

    n_full = (qi * tq) // tk
    cmax0 = tuple(qk(c, 0) for c in range(last))
    p_scr[last][...] = jnp.zeros((tk, tq), BF16)

    def trip(i, carry):
        stats, cmaxs, alpha_last = carry
        stats, cmaxs = list(stats), list(cmaxs)
        cmax_last = qk(last, i)
        pv(last, jnp.maximum(i - 1, 0), alpha_last)
        for c in range(nch):
            stats[c], alpha = softmax(c, i, stats[c], cmaxs[c] if c < last else cmax_last, False)
            if c < last:
                cmaxs[c] = qk(c, i + 1)
                pv(c, i, alpha)
        return tuple(stats), tuple(cmaxs), alpha

    init = (tuple((jnp.full((1, tq), -jnp.inf, F32), jnp.zeros((1, tq), F32)) for _ in chains),
            cmax0, jnp.ones((1, tq), F32))
    stats, _, alpha_last = lax.fori_loop(0, n_full, trip, init)
    stats = list(stats)
    qk(last, n_full)
    pv(last, jnp.maximum(n_full - 1, 0), alpha_last)
    for c in range(nch):
        stats[c], alpha = softmax(c, n_full, stats[c], None, True)
        pv(c, n_full, alpha)

    lam = (jnp.exp(jnp.sum(lq1_ref[...] * lk1_ref[...], axis=-1, keepdims=True))
           - jnp.exp(jnp.sum(lq2_ref[...] * lk2_ref[...], axis=-1, keepdims=True)) + lam_init)
    for hh in range(FLASH_HPB):
        o1 = acc_scr[2 * hh][...] / stats[2 * hh][1]
        o2 = acc_scr[2 * hh + 1][...] / stats[2 * hh + 1][1]
        o = (o1 - lam * o2).T
        ms = jnp.mean(o * o, axis=-1, keepdims=True)
        o_ref[:, hh * LANE:(hh + 1) * LANE] = (
            o * lax.rsqrt(ms + EPS) * gain_ref[...] * (1.0 - lam_init)).astype(o_ref.dtype)


def _flash_diff(dqt, dk, dvt, lq1, lk1, lq2, lk2, gain, b, t, tq, tk, lam_init):
    nq = t // tq
    nkb = t // tk
    qpb = tk // tq
    hw = FLASH_HPB * LANE
    small = lambda s: pl.BlockSpec(s, lambda bi, h, qi: (0,) * len(s))
    return pl.pallas_call(
        functools.partial(_flash_kernel, tq=tq, tk=tk, lam_init=lam_init),
        grid=(b, DIFF_HEADS // FLASH_HPB, nq),
        in_specs=[
            pl.BlockSpec((1, hw, tq), lambda bi, h, qi: (bi * nkb + qi // qpb, h, qi % qpb)),
            pl.BlockSpec((t, hw), lambda bi, h, qi: (bi, h)),
            pl.BlockSpec((nkb, hw, tk), lambda bi, h, qi: (bi, h, 0)),
            small((1, DIFF_DH)), small((1, DIFF_DH)), small((1, DIFF_DH)), small((1, DIFF_DH)),
            small((1, DIFF_DV)),
        ],
        out_specs=pl.BlockSpec((tq, hw), lambda bi, h, qi: (bi * nq + qi, h)),
        out_shape=jax.ShapeDtypeStruct((b * t, DIFF_V), BF16),
        scratch_shapes=([pltpu.VMEM((DIFF_DV, tq), F32)] * (2 * FLASH_HPB)
                        + [pltpu.VMEM((tk, tq), F32)] * (2 * FLASH_HPB)
                        + [pltpu.VMEM((tk, tq), BF16)] * (2 * FLASH_HPB)),
        compiler_params=_cparams(("parallel", "parallel", "arbitrary")),
        name="flash_diff",
    )(dqt, dk, dvt, lq1, lk1, lq2, lk2, gain)


def _gdnprep_kernel(cur_ref, halo_ref, ab_ref, convw_ref, alog_ref, dtb_ref, alogc_ref, dtbc_ref,
                    wq_ref, u_ref, aq_ref, kdt_ref, eg_ref, *, nblk):
    i = pl.program_id(0)
    n = GDN_BLK
    keep = jnp.where(i % nblk == 0, 0.0, 1.0).astype(F32)
    xs = jnp.concatenate([halo_ref[...] * keep, cur_ref[...]], axis=0)
    cw = convw_ref[...]
    conv = xs[8:8 + n] * cw[CONV_K - 1:CONV_K]
    for j in range(CONV_K - 1):
        off = 8 - (CONV_K - 1) + j
        conv = conv + xs[off:off + n] * cw[j:j + 1]
    qkv = _silu(conv)

    ab = ab_ref[...]
    ab_t = ab.T
    g_col = -jnp.exp(alog_ref[...]) * _softplus(ab + dtb_ref[...])
    beta_col = _sigmoid(ab)
    g_row = -jnp.exp(alogc_ref[...]) * _softplus(ab_t[:8] + dtbc_ref[...])

    ri = lax.broadcasted_iota(jnp.int32, (n, n), 0)
    ci = lax.broadcasted_iota(jnp.int32, (n, n), 1)
    same = jnp.right_shift(ri, 6) == jnp.right_shift(ci, 6)
    same32 = jnp.right_shift(ri, 5) == jnp.right_shift(ci, 5)
    same16 = jnp.right_shift(ri, 4) == jnp.right_shift(ci, 4)
    incl = jnp.logical_and(same, ri >= ci)
    strict = jnp.logical_and(same, ri > ci)
    one = jnp.ones((n, n), F32)
    zero = jnp.zeros((n, n), F32)
    tri = jnp.where(incl, one, zero)
    blk_ones = jnp.where(same, one, zero)
    eye = jnp.where(ri == ci, one, zero)
    hdot = lambda a, b: jnp.dot(a, b, precision=HI, preferred_element_type=F32)
    gc_col = hdot(tri, g_col)
    gl_col = hdot(blk_ones, g_col)
    gc_row = lax.dot_general(g_row, tri, (((1,), (1,)), ((), ())), precision=HI,
                             preferred_element_type=F32)

    heads = range(GDN_HEADS)
    qs, ks, kbs, kbfs, decays, egcs, glbs, gcbs, rhss = [], [], [], [], [], [], [], [], []
    for h in heads:
        q = qkv[:, h * GDN_DK:(h + 1) * GDN_DK]
        k = qkv[:, GDN_QK + h * GDN_DK: GDN_QK + (h + 1) * GDN_DK]
        v = qkv[:, 2 * GDN_QK + h * GDN_DV: 2 * GDN_QK + (h + 1) * GDN_DV]
        q = q * lax.rsqrt(jnp.sum(q * q, axis=-1, keepdims=True) + EPS) * (GDN_DK ** -0.5)
        k = k * lax.rsqrt(jnp.sum(k * k, axis=-1, keepdims=True) + EPS)
        beta = beta_col[:, GDN_HEADS + h: GDN_HEADS + h + 1]
        gcb = jnp.broadcast_to(gc_col[:, h:h + 1], (n, GDN_DK))
        glb = jnp.broadcast_to(gl_col[:, h:h + 1], (n, GDN_DK))
        gdiff = jnp.concatenate([gcb, gcb], axis=1) - gc_row[h:h + 1, :]
        kb = k * beta
        egc = jnp.exp(gcb)
        qs.append(q); ks.append(k); kbs.append(kb); kbfs.append(k.astype(BF16))
        decays.append(jnp.where(incl, jnp.exp(jnp.where(incl, gdiff, zero)), zero))
        egcs.append(egc); glbs.append(glb); gcbs.append(gcb)
        rhss.append(jnp.concatenate([v * beta, kb * egc], axis=1).astype(BF16))

    kk = [_dot_nt(kbs[h].astype(BF16), kbfs[h]) for h in heads]
    qk = [_dot_nt(qs[h].astype(BF16), kbfs[h]) for h in heads]
    lowers = [jnp.where(strict, kk[h] * decays[h], zero) for h in heads]
    xs_ = [jnp.where(same16, -lowers[h], zero) for h in heads]
    tmats = [eye + xs_[h] for h in heads]
    for _ in range(int(math.log2(INV_BASE)) - 1):
        xs_ = [_tdot(xs_[h], xs_[h]) for h in heads]
        tmats = [tmats[h] + _tdot(tmats[h], xs_[h]) for h in heads]
    for lo_mask, hi_mask in ((same16, same32), (same32, same)):
        sel = jnp.logical_and(hi_mask, jnp.logical_not(lo_mask))
        tmp = [_tdot(tmats[h], jnp.where(sel, lowers[h], zero)) for h in heads]
        tmats = [tmats[h] - _tdot(tmp[h], tmats[h]) for h in heads]
    uws = [_dot(tmats[h].astype(BF16), rhss[h]) for h in heads]

    for h in heads:
        u = uws[h][:, :GDN_DV]
        w = uws[h][:, GDN_DV:].astype(BF16)
        a_intra = (qk[h] * decays[h]).astype(BF16)
        q_dec = (qs[h] * egcs[h]).astype(BF16)
        kd_t = (ks[h] * jnp.exp(glbs[h] - gcbs[h])).T.astype(BF16)
        eg = jnp.exp(glbs[h])
        for c in range(CPB):
            rs = slice(c * CHUNK, (c + 1) * CHUNK)
            wq_ref[0, h, c, 0:CHUNK, :] = w[rs]
            wq_ref[0, h, c, CHUNK:2 * CHUNK, :] = q_dec[rs]
            u_ref[0, h, c] = u[rs]
            aq_ref[0, h, c] = a_intra[rs, rs]
            kdt_ref[0, h, c] = kd_t[:, rs]
            eg_ref[0, h, c] = eg[c * CHUNK: c * CHUNK + 1, :]


def _gdn_prep(proj, conv_w, a_log_pat, dtb_pat, a_log_col, dtb_col, b, t):
    nblk = t // GDN_BLK
    nc = t // CHUNK
    c3 = 2 * GDN_QK + GDN_V
    small = lambda s: pl.BlockSpec(s, lambda i: (0,) * len(s))
    omap = lambda i: (i // nblk, 0, i % nblk, 0, 0)
    return pl.pallas_call(
        functools.partial(_gdnprep_kernel, nblk=nblk),
        grid=(b * nblk,),
        in_specs=[
            pl.BlockSpec((GDN_BLK, c3), lambda i: (i, COL_QKV // c3)),
            pl.BlockSpec((8, c3), lambda i: (jnp.maximum(i * (GDN_BLK // 8) - 1, 0), COL_QKV // c3)),
            pl.BlockSpec((GDN_BLK, LANE), lambda i: (i, COL_AB // LANE)),
            small((CONV_K, c3)), small((1, LANE)), small((1, LANE)), small((8, 1)), small((8, 1)),
        ],
        out_specs=[
            pl.BlockSpec((1, GDN_HEADS, CPB, 2 * CHUNK, GDN_DK), omap),
            pl.BlockSpec((1, GDN_HEADS, CPB, CHUNK, GDN_DV), omap),
            pl.BlockSpec((1, GDN_HEADS, CPB, CHUNK, CHUNK), omap),
            pl.BlockSpec((1, GDN_HEADS, CPB, GDN_DK, CHUNK), omap),
            pl.BlockSpec((1, GDN_HEADS, CPB, 1, GDN_DV), omap),
        ],
        out_shape=[
            jax.ShapeDtypeStruct((b, GDN_HEADS, nc, 2 * CHUNK, GDN_DK), BF16),
            jax.ShapeDtypeStruct((b, GDN_HEADS, nc, CHUNK, GDN_DV), F32),
            jax.ShapeDtypeStruct((b, GDN_HEADS, nc, CHUNK, CHUNK), BF16),
            jax.ShapeDtypeStruct((b, GDN_HEADS, nc, GDN_DK, CHUNK), BF16),
            jax.ShapeDtypeStruct((b, GDN_HEADS, nc, 1, GDN_DV), F32),
        ],
        compiler_params=_cparams(("parallel",)),
        name="gdn_prep",
    )(proj, proj, proj, conv_w, a_log_pat, dtb_pat, a_log_col, dtb_col)


def _gdnscan_kernel(wq_ref, u_ref, aq_ref, kdt_ref, eg_ref, o_ref, s_scr, *, nb, cb):
    @pl.when(pl.program_id(0) == 0)
    def _():
        s_scr[...] = jnp.zeros(s_scr.shape, F32)

    streams = [(bi, h) for bi in range(nb) for h in range(GDN_HEADS)]

    def chunk(c, carry):
        rs = [_dot(wq_ref[bi, h, c], s_scr[bi, h].astype(BF16)) for bi, h in streams]
        vbs = [(u_ref[bi, h, c] - r[:CHUNK]).astype(BF16) for (bi, h), r in zip(streams, rs)]
        for (bi, h), r, vb in zip(streams, rs, vbs):
            o_ref[bi, h, c] = r[CHUNK:] + _dot(aq_ref[bi, h, c], vb)
        for (bi, h), vb in zip(streams, vbs):
            s_scr[bi, h] = s_scr[bi, h] * eg_ref[bi, h, c] + _dot(kdt_ref[bi, h, c], vb)
        return carry

    lax.fori_loop(0, cb, chunk, 0)


def _gdn_scan(wq, u, aq, kdt, eg, cb):
    b, _, nc = wq.shape[:3]
    spec = lambda d0, d1: pl.BlockSpec((b, GDN_HEADS, cb, d0, d1), lambda i: (0, 0, i, 0, 0))
    return pl.pallas_call(
        functools.partial(_gdnscan_kernel, nb=b, cb=cb),
        grid=(nc // cb,),
        in_specs=[spec(2 * CHUNK, GDN_DK), spec(CHUNK, GDN_DV), spec(CHUNK, CHUNK),
                  spec(GDN_DK, CHUNK), spec(1, GDN_DV)],
        out_specs=spec(CHUNK, GDN_DV),
        out_shape=jax.ShapeDtypeStruct((b, GDN_HEADS, nc, CHUNK, GDN_DV), F32),
        scratch_shapes=[pltpu.VMEM((b, GDN_HEADS, GDN_DK, GDN_DV), F32)],
        compiler_params=_cparams(("arbitrary",)),
        name="gdn_scan",
    )(wq, u, aq, kdt, eg)


def _merge_kernel(x_ref, oa_ref, z_ref, ob_ref, gate_ref, gn_ref, woa_ref, wob_ref, wout_ref, o_ref):
    parts = []
    for h in range(GDN_HEADS):
        o = oa_ref[0, h]
        ms = jnp.mean(o * o, axis=-1, keepdims=True)
        z = z_ref[:, h * GDN_DV:(h + 1) * GDN_DV]
        parts.append((o * lax.rsqrt(ms + EPS) * gn_ref[...] * _silu(z)).astype(BF16))
    y_a = _dot(jnp.concatenate(parts, axis=1), woa_ref[...])
    y_b = _dot(ob_ref[...], wob_ref[...])
    g = gate_ref[...]
    merged = _sigmoid(g[:, :D_MODEL]) * y_a + _sigmoid(g[:, D_MODEL:]) * y_b
    o_ref[...] = x_ref[...] + _dot(merged.astype(BF16), wout_ref[...])


def _merge(x2d, o_a, proj, o_b, gdn_gain, w_o_a, w_o_b, w_out, t, tm):
    m = x2d.shape[0]
    npb = t // tm
    small = lambda s: pl.BlockSpec(s, lambda i: (0,) * len(s))
    return pl.pallas_call(
        _merge_kernel,
        grid=(m // tm,),
        in_specs=[
            pl.BlockSpec((tm, D_MODEL), lambda i: (i, 0)),
            pl.BlockSpec((1, GDN_HEADS, tm, GDN_DV), lambda i: (i // npb, 0, i % npb, 0)),
            pl.BlockSpec((tm, GDN_V), lambda i: (i, COL_Z // GDN_V)),
            pl.BlockSpec((tm, DIFF_V), lambda i: (i, 0)),
            pl.BlockSpec((tm, 2 * D_MODEL), lambda i: (i, COL_GATE // (2 * D_MODEL))),
            small((1, GDN_DV)), small((GDN_V, D_MODEL)), small((DIFF_V, D_MODEL)), small((D_MODEL, D_MODEL)),
        ],
        out_specs=pl.BlockSpec((tm, D_MODEL), lambda i: (i, 0)),
        out_shape=jax.ShapeDtypeStruct((m, D_MODEL), F32),
        compiler_params=_cparams(("parallel",)),
        name="merge",
    )(x2d, o_a, proj, o_b, proj, gdn_gain, w_o_a, w_o_b, w_out)


def _mlp_kernel(x_ref, mg_ref, wup_ref, wdn_ref, pg_ref, wpg_ref, p_ref, wp_ref, o_ref, h_scr, acc_scr):
    f = pl.program_id(1)

    @pl.when(f == 0)
    def _():
        x = x_ref[...]
        ms = jnp.mean(x * x, axis=-1, keepdims=True)
        h_scr[...] = (x * lax.rsqrt(ms + EPS) * mg_ref[...]).astype(BF16)
        acc_scr[...] = jnp.zeros(acc_scr.shape, F32)

    a = jnp.maximum(_dot(h_scr[...], wup_ref[...]), 0.0)
    acc_scr[...] += _dot((a * a).astype(BF16), wdn_ref[...])

    @pl.when(f == pl.num_programs(1) - 1)
    def _():
        x2 = x_ref[...] + acc_scr[...]
        ms = jnp.mean(x2 * x2, axis=-1, keepdims=True)
        hn = (x2 * lax.rsqrt(ms + EPS) * pg_ref[...]).astype(BF16)
        gate = _sigmoid(_dot(hn, wpg_ref[...]))
        o_ref[...] = x2 + gate * _dot(p_ref[...].astype(BF16), wp_ref[...])


def _mlp_ple(x1, mlp_gain, w_up, w_down, ple_gain, w_ple_gate, p2d, w_ple, tm, tf):
    m = x1.shape[0]
    small = lambda s: pl.BlockSpec(s, lambda i, f: (0,) * len(s))
    return pl.pallas_call(
        _mlp_kernel,
        grid=(m // tm, D_FF // tf),
        in_specs=[
            pl.BlockSpec((tm, D_MODEL), lambda i, f: (i, 0)),
            small((1, D_MODEL)),
            pl.BlockSpec((D_MODEL, tf), lambda i, f: (0, f)),
            pl.BlockSpec((tf, D_MODEL), lambda i, f: (f, 0)),
            small((1, D_MODEL)), small((D_MODEL, D_MODEL)),
            pl.BlockSpec((tm, PLE_DIM), lambda i, f: (i, 0)),
            small((PLE_DIM, D_MODEL)),
        ],
        out_specs=pl.BlockSpec((tm, D_MODEL), lambda i, f: (i, 0)),
        out_shape=jax.ShapeDtypeStruct((m, D_MODEL), F32),
        scratch_shapes=[pltpu.VMEM((tm, D_MODEL), BF16), pltpu.VMEM((tm, D_MODEL), F32)],
        compiler_params=_cparams(("parallel", "arbitrary")),
        name="mlp_ple",
    )(x1, mlp_gain, w_up, w_down, ple_gain, w_ple_gate, p2d, w_ple)


def _reorder_w_in(w):
    o_z = 3 * GDN_QK
    o_a = o_z + GDN_V
    o_dq = o_a + 2 * GDN_HEADS
    o_gate = o_dq + 3 * DIFF_QK
    pad = jnp.zeros((D_MODEL, N_PAD - COL_AB - 2 * GDN_HEADS), w.dtype)
    return jnp.concatenate(
        [w[:, :o_a], w[:, o_gate:], w[:, o_dq:o_gate], w[:, o_a:o_dq], pad], axis=1).astype(BF16)


def _lane_pat(vec, offset):
    return jnp.zeros((1, LANE), F32).at[0, offset:offset + GDN_HEADS].set(vec.astype(F32))


def _col_pat(vec):
    return jnp.zeros((8, 1), F32).at[:GDN_HEADS, 0].set(vec.astype(F32))


def _pick(n, pref):
    return pref if n % pref == 0 else n


def kernel(x, p, positions, attn_norm, w_in, conv_w, a_log, dt_bias, gdn_norm, w_o_a, q_norm, k_norm,
           lambda_q1, lambda_k1, lambda_q2, lambda_k2, diff_norm, w_o_b, w_out, mlp_norm, w_up, w_down,
           ple_norm, w_ple_gate, w_ple):
    b, t, _ = x.shape
    m = b * t
    depth = w_in.shape[0]
    assert t % GDN_BLK == 0, "sequence length must be a multiple of the gdn_prep block"

    tm_big = _pick(m, 1024)
    tm_mid = _pick(m, 512)
    tq = _pick(t, 512)
    tk = _pick(t, 512)
    assert tk % tq == 0
    cb = _pick(t // CHUNK, 4)
    pos3 = positions.reshape(m // tk, 1, tk)
    freq = (ROPE_THETA ** (-jnp.arange(0, ROT_DIM, 2, dtype=F32) / ROT_DIM)).reshape(ROT_DIM // 2, 1)
    row = lambda v: v.astype(F32).reshape(1, -1)

    x2d = x.reshape(m, D_MODEL)
    for i in range(depth):
        lam_init = 0.8 - 0.6 * math.exp(-0.3 * i)
        proj = _in_proj(x2d, row(attn_norm[i]), _reorder_w_in(w_in[i]), tm_big, 1024)

        dqt, dk, dvt = _diff_prep(proj, pos3, freq, jnp.tile(row(q_norm[i]), (1, DIFF_QK // DIFF_DH)),
                                  jnp.tile(row(k_norm[i]), (1, DIFF_QK // DIFF_DH)), tk)
        o_b = _flash_diff(dqt, dk, dvt, row(lambda_q1[i]), row(lambda_k1[i]), row(lambda_q2[i]),
                          row(lambda_k2[i]), row(diff_norm[i]), b, t, tq, tk, lam_init)

        wq, u, aq, kdt, eg = _gdn_prep(proj, conv_w[i].astype(F32), _lane_pat(a_log[i], 0),
                                       _lane_pat(dt_bias[i], 0), _col_pat(a_log[i]), _col_pat(dt_bias[i]), b, t)
        o_a = _gdn_scan(wq, u, aq, kdt, eg, cb).reshape(b, GDN_HEADS, t, GDN_DV)

        x1 = _merge(x2d, o_a, proj, o_b, row(gdn_norm[i]), w_o_a[i].astype(BF16), w_o_b[i].astype(BF16),
                    w_out[i].astype(BF16), t, _pick(t, 512))
        x2d = _mlp_ple(x1, row(mlp_norm[i]), w_up[i].astype(BF16), w_down[i].astype(BF16), row(ple_norm[i]),
                       w_ple_gate[i].astype(BF16), p[i].reshape(m, PLE_DIM), w_ple[i].astype(BF16),
                       tm_mid, 1024)
    return x2d.reshape(b, t, D_MODEL)
```

```python
import functools
import math

import jax
import jax.numpy as jnp
from jax import lax
from jax.experimental import pallas as pl
from jax.experimental.pallas import tpu as pltpu

F32 = jnp.float32
BF16 = jnp.bfloat16

D_MODEL = 1024
PLE_DIM = 256
EPS = 1e-6
GDN_HEADS = 4
GDN_DK = 128
GDN_DV = 128
CONV_K = 4
CHUNK = 64
DIFF_HEADS = 4
DIFF_DH = 64
DIFF_DV = 2 * DIFF_DH
ROT_DIM = DIFF_DH // 4
ROPE_THETA = 500000.0
D_FF = 4 * D_MODEL

GDN_QK = GDN_HEADS * GDN_DK
GDN_V = GDN_HEADS * GDN_DV
DIFF_QK = DIFF_HEADS * 2 * DIFF_DH
DIFF_V = DIFF_HEADS * DIFF_DV
D_IN = 4 * GDN_QK + 2 * GDN_HEADS + 3 * DIFF_QK + 2 * D_MODEL

COL_QKV = 0
COL_Z = 1536
COL_GATE = 2048
COL_DQ = 4096
COL_DK = 4608
COL_DV = 5120
COL_AB = 5632
N_PAD = COL_AB + 128

LANE = 128
GDN_BLK = 256
CPB = GDN_BLK // CHUNK
FLASH_HPB = 2
LOG2E = math.log2(math.e)
INV_BASE = 16
INV_PASSES = 1
VMEM_LIMIT = 48 * 1024 * 1024

HI = lax.Precision.HIGHEST


def _cparams(sem):
    return pltpu.CompilerParams(dimension_semantics=sem, vmem_limit_bytes=VMEM_LIMIT)


def _dot(a, b):
    return jnp.dot(a, b, preferred_element_type=F32)


def _dot_nt(a, b):
    return lax.dot_general(a, b, (((1,), (1,)), ((), ())), preferred_element_type=F32)


def _split_bf16(a):
    hi = a.astype(BF16)
    lo = (a - hi.astype(F32)).astype(BF16)
    return hi, lo


def _dot3(a, b):
    ah, al = _split_bf16(a)
    bh, bl = _split_bf16(b)
    return _dot(ah, bh) + _dot(ah, bl) + _dot(al, bh)


def _tdot(a, b):
    if INV_PASSES == 3:
        return _dot3(a, b)
    return _dot(a.astype(BF16), b.astype(BF16))


def _sigmoid(x):
    return 1.0 / (1.0 + jnp.exp(-x))


def _silu(x):
    return x * _sigmoid(x)


def _softplus(x):
    return jnp.maximum(x, 0.0) + jnp.log(1.0 + jnp.exp(-jnp.abs(x)))


def _inproj_kernel(x_ref, g_ref, w_ref, o_ref, *, tn):
    x = x_ref[...]
    ms = jnp.mean(x * x, axis=-1, keepdims=True)
    h = (x * lax.rsqrt(ms + EPS) * g_ref[...]).astype(BF16)
    for lo in range(0, N_PAD, tn):
        hi = min(lo + tn, N_PAD)
        o_ref[:, lo:hi] = _dot(h, w_ref[:, lo:hi])


def _in_proj(x2d, gain, w_bf16, tm, tn):
    m = x2d.shape[0]
    return pl.pallas_call(
        functools.partial(_inproj_kernel, tn=tn),
        grid=(m // tm,),
        in_specs=[
            pl.BlockSpec((tm, D_MODEL), lambda i: (i, 0)),
            pl.BlockSpec((1, D_MODEL), lambda i: (0, 0)),
            pl.BlockSpec((D_MODEL, N_PAD), lambda i: (0, 0), pipeline_mode=pl.Buffered(1)),
        ],
        out_specs=pl.BlockSpec((tm, N_PAD), lambda i: (i, 0)),
        out_shape=jax.ShapeDtypeStruct((m, N_PAD), F32),
        compiler_params=_cparams(("parallel",)),
        name="in_proj",
    )(x2d, gain, w_bf16)


def _diffprep_kernel(pos_ref, freq_ref, q_ref, k_ref, v_ref, qg_ref, kg_ref, oq_ref, ok_ref, ov_ref):
    tm = q_ref.shape[0]
    ang = freq_ref[...] * pos_ref[0].astype(F32)
    cos_t, sin_t = jnp.cos(ang), jnp.sin(ang)
    half = ROT_DIM // 2
    fi = jnp.bitwise_and(lax.broadcasted_iota(jnp.int32, (3 * half, LANE), 0), half - 1)
    d = jnp.bitwise_and(lax.broadcasted_iota(jnp.int32, (3 * half, LANE), 1), DIFF_DH - 1)
    e_lo = jnp.where(d == fi, 1.0, 0.0).astype(BF16)
    e_hi = jnp.where(d == fi + half, 1.0, 0.0).astype(BF16)

    def expand(table_t, sel):
        p0 = table_t.astype(BF16).astype(F32)
        r1 = table_t - p0
        p1 = r1.astype(BF16).astype(F32)
        p2 = (r1 - p1).astype(BF16).astype(F32)
        pieces = jnp.concatenate([p0, p1, p2], axis=0).astype(BF16)
        out = lax.dot_general(pieces, sel, (((0,), (0,)), ((), ())), preferred_element_type=F32)
        return jnp.concatenate([out] * (DIFF_QK // LANE), axis=1)

    lane = lax.broadcasted_iota(jnp.int32, (1, DIFF_QK), 1)
    passthrough = jnp.where(jnp.bitwise_and(lane, DIFF_DH - 1) >= ROT_DIM, 1.0, 0.0).astype(F32)
    cos_f = expand(cos_t, e_lo + e_hi) + passthrough
    sin_a = -expand(sin_t, e_lo)
    sin_b = expand(sin_t, e_hi)

    ri = lax.broadcasted_iota(jnp.int32, (DIFF_QK, DIFF_QK), 0)
    ci = lax.broadcasted_iota(jnp.int32, (DIFF_QK, DIFF_QK), 1)
    grp = jnp.where(jnp.right_shift(ri, 6) == jnp.right_shift(ci, 6), 1.0 / DIFF_DH, 0.0).astype(BF16)

    def norm_rot(x, gain):
        sq = x * x
        hi, lo = _split_bf16(sq)
        ms = _dot(hi, grp) + _dot(lo, grp)
        y = x * lax.rsqrt(ms + EPS) * gain
        up = pltpu.roll(y, DIFF_QK - half, 1)
        dn = pltpu.roll(y, half, 1)
        return y * cos_f + up * sin_a + dn * sin_b

    oq_ref[0] = norm_rot(q_ref[...], qg_ref[...] * (DIFF_DH ** -0.5 * LOG2E)).T.astype(BF16)
    ok_ref[...] = norm_rot(k_ref[...], kg_ref[...]).astype(BF16)
    ov_ref[0] = v_ref[...].T.astype(BF16)


def _diff_prep(proj, pos3, freq, qg, kg, tm):
    m = proj.shape[0]
    blk = lambda c: pl.BlockSpec((tm, DIFF_QK), lambda i: (i, c))
    small = lambda s: pl.BlockSpec(s, lambda i: (0,) * len(s))
    out_t = jax.ShapeDtypeStruct((m // tm, DIFF_QK, tm), BF16)
    spec_t = pl.BlockSpec((1, DIFF_QK, tm), lambda i: (i, 0, 0))
    return pl.pallas_call(
        _diffprep_kernel,
        grid=(m // tm,),
        in_specs=[
            pl.BlockSpec((1, 1, tm), lambda i: (i, 0, 0)),
            small((ROT_DIM // 2, 1)),
            blk(COL_DQ // DIFF_QK), blk(COL_DK // DIFF_QK), blk(COL_DV // DIFF_QK),
            small((1, DIFF_QK)), small((1, DIFF_QK)),
        ],
        out_specs=[spec_t, pl.BlockSpec((tm, DIFF_QK), lambda i: (i, 0)), spec_t],
        out_shape=[out_t, jax.ShapeDtypeStruct((m, DIFF_QK), BF16), out_t],
        compiler_params=_cparams(("parallel",)),
        name="diff_prep",
    )(pos3, freq, proj, proj, proj, qg, kg)


def _flash_kernel(qt_ref, k_ref, vt_ref, lq1_ref, lk1_ref, lq2_ref, lk2_ref, gain_ref, o_ref,
                  *scratch, tq, tk, lam_init):
    qi = pl.program_id(2)
    chains = [(hh, c) for hh in range(FLASH_HPB) for c in range(2)]
    acc_scr, s_scr, p_scr = (scratch[k * len(chains):(k + 1) * len(chains)] for k in range(3))
    qst = {}
    for hh in range(FLASH_HPB):
        qt = qt_ref[0, hh * LANE:(hh + 1) * LANE, :]
        row = lax.broadcasted_iota(jnp.int32, qt.shape, 0)
        zero = jnp.zeros_like(qt)
        qst[hh, 0] = jnp.where(row < DIFF_DH, qt, zero)
        qst[hh, 1] = jnp.where(row >= DIFF_DH, qt, zero)

    nch = len(chains)
    last = nch - 1
    for c in range(nch):
        acc_scr[c][...] = jnp.zeros((DIFF_DV, tq), F32)

    def qk(c, j):
        hh, comp = chains[c]
        start = pl.multiple_of(j * tk, tk)
        s = _dot(k_ref[pl.ds(start, tk), hh * LANE:(hh + 1) * LANE], qst[hh, comp])
        s_scr[c][...] = s
        return jnp.max(s, axis=0, keepdims=True)

    def pv(c, j, alpha):
        hh, comp = chains[c]
        upd = _dot(vt_ref[j, hh * LANE:(hh + 1) * LANE, :], p_scr[c][...])
        acc_scr[c][...] = alpha * acc_scr[c][...] + upd

    def softmax(c, j, stat, cmax, masked):
        m_old, l_old = stat
        s = s_scr[c][...]
        if masked:
            r = lax.broadcasted_iota(jnp.int32, (tk, tq), 0)
            col = lax.broadcasted_iota(jnp.int32, (tk, tq), 1)
            s = jnp.where(j * tk + r <= qi * tq + col, s, -jnp.inf)
            cmax = jnp.max(s, axis=0, keepdims=True)
        m_new = jnp.maximum(m_old, cmax)
        alpha = jnp.exp2(m_old - m_new)
        p = jnp.exp2(s - m_new)
        p_scr[c][...] = p.astype(BF16)
        return (m_new, alpha * l_old + jnp.sum(p, axis=0, keepdims=True)), alpha

    n_full = (qi * tq) // tk
    cmax0 = tuple(qk(c, 0) for c in range(last))
    p_scr[last][...] = jnp.zeros((tk, tq), BF16)

    def trip(i, carry):
        stats, cmaxs, alpha_last = carry
        stats, cmaxs = list(stats), list(cmaxs)
        cmax_last = qk(last, i)
        pv(last, jnp.maximum(i - 1, 0), alpha_last)
        for c in range(nch):
            stats[c], alpha = softmax(c, i, stats[c], cmaxs[c] if c < last else cmax_last, False)
            if c < last:
                cmaxs[c] = qk(c, i + 1)
                pv(c, i, alpha)
        return tuple(stats), tuple(cmaxs), alpha

    init = (tuple((jnp.full((1, tq), -jnp.inf, F32), jnp.zeros((1, tq), F32)) for _ in chains),
            cmax0, jnp.ones((1, tq), F32))
    stats, _, alpha_last = lax.fori_loop(0, n_full, trip, init)
    stats = list(stats)
    qk(last, n_full)
    pv(last, jnp.maximum(n_full - 1, 0), alpha_last)
    for c in range(nch):
        stats[c], alpha = softmax(c, n_full, stats[c], None, True)
        pv(c, n_full, alpha)

    lam = (jnp.exp(jnp.sum(lq1_ref[...] * lk1_ref[...], axis=-1, keepdims=True))
           - jnp.exp(jnp.sum(lq2_ref[...] * lk2_ref[...], axis=-1, keepdims=True)) + lam_init)
    for hh in range(FLASH_HPB):
        o1 = acc_scr[2 * hh][...] / stats[2 * hh][1]
        o2 = acc_scr[2 * hh + 1][...] / stats[2 * hh + 1][1]
        o = (o1 - lam * o2).T
        ms = jnp.mean(o * o, axis=-1, keepdims=True)
        o_ref[:, hh * LANE:(hh + 1) * LANE] = (
            o * lax.rsqrt(ms + EPS) * gain_ref[...] * (1.0 - lam_init)).astype(o_ref.dtype)


def _flash_diff(dqt, dk, dvt, lq1, lk1, lq2, lk2, gain, b, t, tq, tk, lam_init):
    nq = t // tq
    nkb = t // tk
    qpb = tk // tq
    hw = FLASH_HPB * LANE
    small = lambda s: pl.BlockSpec(s, lambda bi, h, qi: (0,) * len(s))
    return pl.pallas_call(
        functools.partial(_flash_kernel, tq=tq, tk=tk, lam_init=lam_init),
        grid=(b, DIFF_HEADS // FLASH_HPB, nq),
        in_specs=[
            pl.BlockSpec((1, hw, tq), lambda bi, h, qi: (bi * nkb + qi // qpb, h, qi % qpb)),
            pl.BlockSpec((t, hw), lambda bi, h, qi: (bi, h)),
            pl.BlockSpec((nkb, hw, tk), lambda bi, h, qi: (bi, h, 0)),
            small((1, DIFF_DH)), small((1, DIFF_DH)), small((1, DIFF_DH)), small((1, DIFF_DH)),
            small((1, DIFF_DV)),
        ],
        out_specs=pl.BlockSpec((tq, hw), lambda bi, h, qi: (bi * nq + qi, h)),
        out_shape=jax.ShapeDtypeStruct((b * t, DIFF_V), BF16),
        scratch_shapes=([pltpu.VMEM((DIFF_DV, tq), F32)] * (2 * FLASH_HPB)
                        + [pltpu.VMEM((tk, tq), F32)] * (2 * FLASH_HPB)
                        + [pltpu.VMEM((tk, tq), BF16)] * (2 * FLASH_HPB)),
        compiler_params=_cparams(("parallel", "parallel", "arbitrary")),
        name="flash_diff",
    )(dqt, dk, dvt, lq1, lk1, lq2, lk2, gain)


def _gdnprep_kernel(cur_ref, halo_ref, ab_ref, convw_ref, alog_ref, dtb_ref, alogc_ref, dtbc_ref,
                    wq_ref, u_ref, aq_ref, kdt_ref, eg_ref, *, nblk):
    i = pl.program_id(0)
    n = GDN_BLK
    keep = jnp.where(i % nblk == 0, 0.0, 1.0).astype(F32)
    xs = jnp.concatenate([halo_ref[...] * keep, cur_ref[...]], axis=0)
    cw = convw_ref[...]
    conv = xs[8:8 + n] * cw[CONV_K - 1:CONV_K]
    for j in range(CONV_K - 1):
        off = 8 - (CONV_K - 1) + j
        conv = conv + xs[off:off + n] * cw[j:j + 1]
    qkv = _silu(conv)

    ab = ab_ref[...]
    ab_t = ab.T
    g_col = -jnp.exp(alog_ref[...]) * _softplus(ab + dtb_ref[...])
    beta_col = _sigmoid(ab)
    g_row = -jnp.exp(alogc_ref[...]) * _softplus(ab_t[:8] + dtbc_ref[...])

    sb_n = LANE
    ri = lax.broadcasted_iota(jnp.int32, (sb_n, sb_n), 0)
    ci = lax.broadcasted_iota(jnp.int32, (sb_n, sb_n), 1)
    same = jnp.right_shift(ri, 6) == jnp.right_shift(ci, 6)
    same32 = jnp.right_shift(ri, 5) == jnp.right_shift(ci, 5)
    same16 = jnp.right_shift(ri, 4) == jnp.right_shift(ci, 4)
    incl = jnp.logical_and(same, ri >= ci)
    strict = jnp.logical_and(same, ri > ci)
    one = jnp.ones((sb_n, sb_n), F32)
    zero = jnp.zeros((sb_n, sb_n), F32)
    tri = jnp.where(incl, one, zero).astype(BF16)
    blk_ones = jnp.where(same, one, zero).astype(BF16)
    eye = jnp.where(ri == ci, one, zero)

    def split3(a):
        p0 = a.astype(BF16)
        r1 = a - p0.astype(F32)
        p1 = r1.astype(BF16)
        return p0, p1, (r1 - p1.astype(F32)).astype(BF16)

    units = [(h, sb) for sb in range(n // sb_n) for h in range(GDN_HEADS)]
    gc_cols, gl_cols, gc_rows = [], [], []
    for sb in range(n // sb_n):
        rows = slice(sb * sb_n, (sb + 1) * sb_n)
        pc = split3(g_col[rows])
        pr = split3(g_row[:, rows])
        gc_cols.append(sum(_dot(tri, p) for p in pc))
        gl_cols.append(sum(_dot(blk_ones, p) for p in pc))
        gc_rows.append(sum(_dot_nt(p, tri) for p in pr))

    qs, ks, kbs, kbfs, decays, egcs, glbs, gcbs, rhss = [], [], [], [], [], [], [], [], []
    for h, sb in units:
        rows = slice(sb * sb_n, (sb + 1) * sb_n)
        q = qkv[rows, h * GDN_DK:(h + 1) * GDN_DK]
        k = qkv[rows, GDN_QK + h * GDN_DK: GDN_QK + (h + 1) * GDN_DK]
        v = qkv[rows, 2 * GDN_QK + h * GDN_DV: 2 * GDN_QK + (h + 1) * GDN_DV]
        q = q * lax.rsqrt(jnp.sum(q * q, axis=-1, keepdims=True) + EPS) * (GDN_DK ** -0.5)
        k = k * lax.rsqrt(jnp.sum(k * k, axis=-1, keepdims=True) + EPS)
        beta = beta_col[rows, GDN_HEADS + h: GDN_HEADS + h + 1]
        gcb = jnp.broadcast_to(gc_cols[sb][:, h:h + 1], (sb_n, GDN_DK))
        glb = jnp.broadcast_to(gl_cols[sb][:, h:h + 1], (sb_n, GDN_DK))
        gdiff = gcb - gc_rows[sb][h:h + 1, :]
        kb = k * beta
        egc = jnp.exp(gcb)
        qs.append(q); ks.append(k); kbs.append(kb); kbfs.append(k.astype(BF16))
        decays.append(jnp.where(incl, jnp.exp(jnp.where(incl, gdiff, zero)), zero))
        egcs.append(egc); glbs.append(glb); gcbs.append(gcb)
        rhss.append(jnp.concatenate([v * beta, kb * egc], axis=1).astype(BF16))

    nu = range(len(units))
    kq = [_dot_nt(jnp.concatenate([kbs[i], qs[i]], axis=0).astype(BF16), kbfs[i]) for i in nu]
    lowers = [jnp.where(strict, kq[i][:sb_n] * decays[i], zero) for i in nu]
    xs_ = [jnp.where(same16, -lowers[i], zero) for i in nu]
    tmats = [eye + xs_[i] for i in nu]
    for _ in range(int(math.log2(INV_BASE)) - 1):
        xs_ = [_tdot(xs_[i], xs_[i]) for i in nu]
        tmats = [tmats[i] + _tdot(tmats[i], xs_[i]) for i in nu]
    for lo_mask, hi_mask in ((same16, same32), (same32, same)):
        sel = jnp.logical_and(hi_mask, jnp.logical_not(lo_mask))
        tmp = [_tdot(tmats[i], jnp.where(sel, lowers[i], zero)) for i in nu]
        tmats = [tmats[i] - _tdot(tmp[i], tmats[i]) for i in nu]
    uws = [_dot(tmats[i].astype(BF16), rhss[i]) for i in nu]

    cps = sb_n // CHUNK
    for i, (h, sb) in enumerate(units):
        u = uws[i][:, :GDN_DV]
        w = uws[i][:, GDN_DV:].astype(BF16)
        a_intra = (kq[i][sb_n:] * decays[i]).astype(BF16)
        q_dec = (qs[i] * egcs[i]).astype(BF16)
        kd_t = (ks[i] * jnp.exp(glbs[i] - gcbs[i])).T.astype(BF16)
        eg = jnp.exp(glbs[i])
        for cc in range(cps):
            c = sb * cps + cc
            rs = slice(cc * CHUNK, (cc + 1) * CHUNK)
            wq_ref[0, h, c, 0:CHUNK, :] = w[rs]
            wq_ref[0, h, c, CHUNK:2 * CHUNK, :] = q_dec[rs]
            u_ref[0, h, c] = u[rs]
            aq_ref[0, h, c] = a_intra[rs, rs]
            kdt_ref[0, h, c] = kd_t[:, rs]
            eg_ref[0, h, c] = eg[cc * CHUNK: cc * CHUNK + 1, :]


def _gdn_prep(proj, conv_w, a_log_pat, dtb_pat, a_log_col, dtb_col, b, t):
    nblk = t // GDN_BLK
    nc = t // CHUNK
    c3 = 2 * GDN_QK + GDN_V
    small = lambda s: pl.BlockSpec(s, lambda i: (0,) * len(s))
    omap = lambda i: (i // nblk, 0, i % nblk, 0, 0)
    return pl.pallas_call(
        functools.partial(_gdnprep_kernel, nblk=nblk),
        grid=(b * nblk,),
        in_specs=[
            pl.BlockSpec((GDN_BLK, c3), lambda i: (i, COL_QKV // c3)),
            pl.BlockSpec((8, c3), lambda i: (jnp.maximum(i * (GDN_BLK // 8) - 1, 0), COL_QKV // c3)),
            pl.BlockSpec((GDN_BLK, LANE), lambda i: (i, COL_AB // LANE)),
            small((CONV_K, c3)), small((1, LANE)), small((1, LANE)), small((8, 1)), small((8, 1)),
        ],
        out_specs=[
            pl.BlockSpec((1, GDN_HEADS, CPB, 2 * CHUNK, GDN_DK), omap),
            pl.BlockSpec((1, GDN_HEADS, CPB, CHUNK, GDN_DV), omap),
            pl.BlockSpec((1, GDN_HEADS, CPB, CHUNK, CHUNK), omap),
            pl.BlockSpec((1, GDN_HEADS, CPB, GDN_DK, CHUNK), omap),
            pl.BlockSpec((1, GDN_HEADS, CPB, 1, GDN_DV), omap),
        ],
        out_shape=[
            jax.ShapeDtypeStruct((b, GDN_HEADS, nc, 2 * CHUNK, GDN_DK), BF16),
            jax.ShapeDtypeStruct((b, GDN_HEADS, nc, CHUNK, GDN_DV), F32),
            jax.ShapeDtypeStruct((b, GDN_HEADS, nc, CHUNK, CHUNK), BF16),
            jax.ShapeDtypeStruct((b, GDN_HEADS, nc, GDN_DK, CHUNK), BF16),
            jax.ShapeDtypeStruct((b, GDN_HEADS, nc, 1, GDN_DV), F32),
        ],
        compiler_params=_cparams(("parallel",)),
        name="gdn_prep",
    )(proj, proj, proj, conv_w, a_log_pat, dtb_pat, a_log_col, dtb_col)


def _gdnscan_kernel(wq_ref, u_ref, aq_ref, kdt_ref, eg_ref, o_ref, s_scr, *, nb, cb):
    @pl.when(pl.program_id(0) == 0)
    def _():
        s_scr[...] = jnp.zeros(s_scr.shape, F32)

    streams = [(bi, h) for bi in range(nb) for h in range(GDN_HEADS)]

    def chunk(c, carry):
        rs = [_dot(wq_ref[bi, h, c], s_scr[bi, h].astype(BF16)) for bi, h in streams]
        vbs = [(u_ref[bi, h, c] - r[:CHUNK]).astype(BF16) for (bi, h), r in zip(streams, rs)]
        for (bi, h), r, vb in zip(streams, rs, vbs):
            o_ref[bi, h, c] = r[CHUNK:] + _dot(aq_ref[bi, h, c], vb)
        for (bi, h), vb in zip(streams, vbs):
            s_scr[bi, h] = s_scr[bi, h] * eg_ref[bi, h, c] + _dot(kdt_ref[bi, h, c], vb)
        return carry

    lax.fori_loop(0, cb, chunk, 0)


def _gdn_scan(wq, u, aq, kdt, eg, cb):
    b, _, nc = wq.shape[:3]
    spec = lambda d0, d1: pl.BlockSpec((b, GDN_HEADS, cb, d0, d1), lambda i: (0, 0, i, 0, 0))
    return pl.pallas_call(
        functools.partial(_gdnscan_kernel, nb=b, cb=cb),
        grid=(nc // cb,),
        in_specs=[spec(2 * CHUNK, GDN_DK), spec(CHUNK, GDN_DV), spec(CHUNK, CHUNK),
                  spec(GDN_DK, CHUNK), spec(1, GDN_DV)],
        out_specs=spec(CHUNK, GDN_DV),
        out_shape=jax.ShapeDtypeStruct((b, GDN_HEADS, nc, CHUNK, GDN_DV), F32),
        scratch_shapes=[pltpu.VMEM((b, GDN_HEADS, GDN_DK, GDN_DV), F32)],
        compiler_params=_cparams(("arbitrary",)),
        name="gdn_scan",
    )(wq, u, aq, kdt, eg)


def _merge_kernel(x_ref, oa_ref, z_ref, ob_ref, gate_ref, gn_ref, woa_ref, wob_ref, wout_ref, o_ref):
    parts = []
    for h in range(GDN_HEADS):
        o = oa_ref[0, h]
        ms = jnp.mean(o * o, axis=-1, keepdims=True)
        z = z_ref[:, h * GDN_DV:(h + 1) * GDN_DV]
        parts.append((o * lax.rsqrt(ms + EPS) * gn_ref[...] * _silu(z)).astype(BF16))
    y_a = _dot(jnp.concatenate(parts, axis=1), woa_ref[...])
    y_b = _dot(ob_ref[...], wob_ref[...])
    g = gate_ref[...]
    merged = _sigmoid(g[:, :D_MODEL]) * y_a + _sigmoid(g[:, D_MODEL:]) * y_b
    o_ref[...] = x_ref[...] + _dot(merged.astype(BF16), wout_ref[...])


def _merge(x2d, o_a, proj, o_b, gdn_gain, w_o_a, w_o_b, w_out, t, tm):
    m = x2d.shape[0]
    npb = t // tm
    small = lambda s: pl.BlockSpec(s, lambda i: (0,) * len(s))
    return pl.pallas_call(
        _merge_kernel,
        grid=(m // tm,),
        in_specs=[
            pl.BlockSpec((tm, D_MODEL), lambda i: (i, 0)),
            pl.BlockSpec((1, GDN_HEADS, tm, GDN_DV), lambda i: (i // npb, 0, i % npb, 0)),
            pl.BlockSpec((tm, GDN_V), lambda i: (i, COL_Z // GDN_V)),
            pl.BlockSpec((tm, DIFF_V), lambda i: (i, 0)),
            pl.BlockSpec((tm, 2 * D_MODEL), lambda i: (i, COL_GATE // (2 * D_MODEL))),
            small((1, GDN_DV)), small((GDN_V, D_MODEL)), small((DIFF_V, D_MODEL)), small((D_MODEL, D_MODEL)),
        ],
        out_specs=pl.BlockSpec((tm, D_MODEL), lambda i: (i, 0)),
        out_shape=jax.ShapeDtypeStruct((m, D_MODEL), F32),
        compiler_params=_cparams(("parallel",)),
        name="merge",
    )(x2d, o_a, proj, o_b, proj, gdn_gain, w_o_a, w_o_b, w_out)


def _mlp_kernel(x_ref, mg_ref, wup_ref, wdn_ref, pg_ref, wpg_ref, p_ref, wp_ref, o_ref, h_scr, acc_scr):
    f = pl.program_id(1)

    @pl.when(f == 0)
    def _():
        x = x_ref[...]
        ms = jnp.mean(x * x, axis=-1, keepdims=True)
        h_scr[...] = (x * lax.rsqrt(ms + EPS) * mg_ref[...]).astype(BF16)
        acc_scr[...] = jnp.zeros(acc_scr.shape, F32)

    a = jnp.maximum(_dot(h_scr[...], wup_ref[...]), 0.0)
    acc_scr[...] += _dot((a * a).astype(BF16), wdn_ref[...])

    @pl.when(f == pl.num_programs(1) - 1)
    def _():
        x2 = x_ref[...] + acc_scr[...]
        ms = jnp.mean(x2 * x2, axis=-1, keepdims=True)
        hn = (x2 * lax.rsqrt(ms + EPS) * pg_ref[...]).astype(BF16)
        gate = _sigmoid(_dot(hn, wpg_ref[...]))
        o_ref[...] = x2 + gate * _dot(p_ref[...].astype(BF16), wp_ref[...])


def _mlp_ple(x1, mlp_gain, w_up, w_down, ple_gain, w_ple_gate, p2d, w_ple, tm, tf):
    m = x1.shape[0]
    small = lambda s: pl.BlockSpec(s, lambda i, f: (0,) * len(s))
    return pl.pallas_call(
        _mlp_kernel,
        grid=(m // tm, D_FF // tf),
        in_specs=[
            pl.BlockSpec((tm, D_MODEL), lambda i, f: (i, 0)),
            small((1, D_MODEL)),
            pl.BlockSpec((D_MODEL, tf), lambda i, f: (0, f)),
            pl.BlockSpec((tf, D_MODEL), lambda i, f: (f, 0)),
            small((1, D_MODEL)), small((D_MODEL, D_MODEL)),
            pl.BlockSpec((tm, PLE_DIM), lambda i, f: (i, 0)),
            small((PLE_DIM, D_MODEL)),
        ],
        out_specs=pl.BlockSpec((tm, D_MODEL), lambda i, f: (i, 0)),
        out_shape=jax.ShapeDtypeStruct((m, D_MODEL), F32),
        scratch_shapes=[pltpu.VMEM((tm, D_MODEL), BF16), pltpu.VMEM((tm, D_MODEL), F32)],
        compiler_params=_cparams(("parallel", "arbitrary")),
        name="mlp_ple",
    )(x1, mlp_gain, w_up, w_down, ple_gain, w_ple_gate, p2d, w_ple)


def _reorder_w_in(w):
    o_z = 3 * GDN_QK
    o_a = o_z + GDN_V
    o_dq = o_a + 2 * GDN_HEADS
    o_gate = o_dq + 3 * DIFF_QK
    pad = jnp.zeros((D_MODEL, N_PAD - COL_AB - 2 * GDN_HEADS), w.dtype)
    return jnp.concatenate(
        [w[:, :o_a], w[:, o_gate:], w[:, o_dq:o_gate], w[:, o_a:o_dq], pad], axis=1).astype(BF16)


def _lane_pat(vec, offset):
    return jnp.zeros((1, LANE), F32).at[0, offset:offset + GDN_HEADS].set(vec.astype(F32))


def _col_pat(vec):
    return jnp.zeros((8, 1), F32).at[:GDN_HEADS, 0].set(vec.astype(F32))


def _pick(n, pref):
    return pref if n % pref == 0 else n


def kernel(x, p, positions, attn_norm, w_in, conv_w, a_log, dt_bias, gdn_norm, w_o_a, q_norm, k_norm,
           lambda_q1, lambda_k1, lambda_q2, lambda_k2, diff_norm, w_o_b, w_out, mlp_norm, w_up, w_down,
           ple_norm, w_ple_gate, w_ple):
    b, t, _ = x.shape
    m = b * t
    depth = w_in.shape[0]
    assert t % GDN_BLK == 0, "sequence length must be a multiple of the gdn_prep block"

    tm_big = _pick(m, 1024)
    tm_mid = _pick(m, 512)
    tq = _pick(t, 512)
    tk = _pick(t, 512)
    assert tk % tq == 0
    cb = _pick(t // CHUNK, 4)
    pos3 = positions.reshape(m // tk, 1, tk)
    freq = (ROPE_THETA ** (-jnp.arange(0, ROT_DIM, 2, dtype=F32) / ROT_DIM)).reshape(ROT_DIM // 2, 1)
    row = lambda v: v.astype(F32).reshape(1, -1)

    x2d = x.reshape(m, D_MODEL)
    for i in range(depth):
        lam_init = 0.8 - 0.6 * math.exp(-0.3 * i)
        proj = _in_proj(x2d, row(attn_norm[i]), _reorder_w_in(w_in[i]), tm_mid, 512)

        dqt, dk, dvt = _diff_prep(proj, pos3, freq, jnp.tile(row(q_norm[i]), (1, DIFF_QK // DIFF_DH)),
                                  jnp.tile(row(k_norm[i]), (1, DIFF_QK // DIFF_DH)), tk)
        o_b = _flash_diff(dqt, dk, dvt, row(lambda_q1[i]), row(lambda_k1[i]), row(lambda_q2[i]),
                          row(lambda_k2[i]), row(diff_norm[i]), b, t, tq, tk, lam_init)

        wq, u, aq, kdt, eg = _gdn_prep(proj, conv_w[i].astype(F32), _lane_pat(a_log[i], 0),
                                       _lane_pat(dt_bias[i], 0), _col_pat(a_log[i]), _col_pat(dt_bias[i]), b, t)
        o_a = _gdn_scan(wq, u, aq, kdt, eg, cb).reshape(b, GDN_HEADS, t, GDN_DV)

        x1 = _merge(x2d, o_a, proj, o_b, row(gdn_norm[i]), w_o_a[i].astype(BF16), w_o_b[i].astype(BF16),
                    w_out[i].astype(BF16), t, _pick(t, 512))
        x2d = _mlp_ple(x1, row(mlp_norm[i]), w_up[i].astype(BF16), w_down[i].astype(BF16), row(ple_norm[i]),
                       w_ple_gate[i].astype(BF16), p[i].reshape(m, PLE_DIM), w_ple[i].astype(BF16),
                       tm_big, 1024)
    return x2d.reshape(b, t, D_MODEL)
```

```python
import functools
import math

import jax
import jax.numpy as jnp
from jax import lax
from jax.experimental import pallas as pl
from jax.experimental.pallas import tpu as pltpu

F32 = jnp.float32
BF16 = jnp.bfloat16

D_MODEL = 1024
PLE_DIM = 256
EPS = 1e-6
GDN_HEADS = 4
GDN_DK = 128
GDN_DV = 128
CONV_K = 4
CHUNK = 64
DIFF_HEADS = 4
DIFF_DH = 64
DIFF_DV = 2 * DIFF_DH
ROT_DIM = DIFF_DH // 4
ROPE_THETA = 500000.0
D_FF = 4 * D_MODEL

GDN_QK = GDN_HEADS * GDN_DK
GDN_V = GDN_HEADS * GDN_DV
DIFF_QK = DIFF_HEADS * 2 * DIFF_DH
DIFF_V = DIFF_HEADS * DIFF_DV
D_IN = 4 * GDN_QK + 2 * GDN_HEADS + 3 * DIFF_QK + 2 * D_MODEL

COL_QKV = 0
COL_DQ = 1536
COL_DK = 2048
COL_DV = 2560
COL_AB = 3072
N_PAD = COL_AB + 128
N_GZ = GDN_V + 2 * D_MODEL

LANE = 128
GDN_BLK = 256
CPB = GDN_BLK // CHUNK
FLASH_HPB = 2
LOG2E = math.log2(math.e)
ACC_ROWS = DIFF_DV + 16
INV_BASE = 16
INV_PASSES = 1
VMEM_LIMIT = 48 * 1024 * 1024

HI = lax.Precision.HIGHEST


def _cparams(sem):
    return pltpu.CompilerParams(dimension_semantics=sem, vmem_limit_bytes=VMEM_LIMIT)


def _dot(a, b):
    return jnp.dot(a, b, preferred_element_type=F32)


def _dot_nt(a, b):
    return lax.dot_general(a, b, (((1,), (1,)), ((), ())), preferred_element_type=F32)


def _split_bf16(a):
    hi = a.astype(BF16)
    lo = (a - hi.astype(F32)).astype(BF16)
    return hi, lo


def _dot3(a, b):
    ah, al = _split_bf16(a)
    bh, bl = _split_bf16(b)
    return _dot(ah, bh) + _dot(ah, bl) + _dot(al, bh)


def _tdot(a, b):
    if INV_PASSES == 3:
        return _dot3(a, b)
    return _dot(a.astype(BF16), b.astype(BF16))


def _sigmoid(x):
    return 1.0 / (1.0 + jnp.exp(-x))


def _silu(x):
    return x * _sigmoid(x)


def _softplus(x):
    return jnp.maximum(x, 0.0) + jnp.log(1.0 + jnp.exp(-jnp.abs(x)))


def _inproj_kernel(x_ref, g_ref, w_ref, o_ref, *, tn):
    x = x_ref[...]
    ms = jnp.mean(x * x, axis=-1, keepdims=True)
    h = (x * lax.rsqrt(ms + EPS) * g_ref[...]).astype(BF16)
    for lo in range(0, N_PAD, tn):
        hi = min(lo + tn, N_PAD)
        o_ref[:, lo:hi] = _dot(h, w_ref[:, lo:hi])


def _in_proj(x2d, gain, w_bf16, tm, tn):
    m = x2d.shape[0]
    return pl.pallas_call(
        functools.partial(_inproj_kernel, tn=tn),
        grid=(m // tm,),
        in_specs=[
            pl.BlockSpec((tm, D_MODEL), lambda i: (i, 0)),
            pl.BlockSpec((1, D_MODEL), lambda i: (0, 0)),
            pl.BlockSpec((D_MODEL, N_PAD), lambda i: (0, 0), pipeline_mode=pl.Buffered(1)),
        ],
        out_specs=pl.BlockSpec((tm, N_PAD), lambda i: (i, 0)),
        out_shape=jax.ShapeDtypeStruct((m, N_PAD), F32),
        compiler_params=_cparams(("parallel",)),
        name="in_proj",
    )(x2d, gain, w_bf16)


def _diffprep_kernel(pos_ref, freq_ref, q_ref, k_ref, v_ref, qg_ref, kg_ref, oq_ref, ok_ref, ov_ref):
    tm = q_ref.shape[0]
    ang = freq_ref[...] * pos_ref[0].astype(F32)
    cos_t, sin_t = jnp.cos(ang), jnp.sin(ang)
    half = ROT_DIM // 2
    fi = jnp.bitwise_and(lax.broadcasted_iota(jnp.int32, (3 * half, LANE), 0), half - 1)
    d = jnp.bitwise_and(lax.broadcasted_iota(jnp.int32, (3 * half, LANE), 1), DIFF_DH - 1)
    e_lo = jnp.where(d == fi, 1.0, 0.0).astype(BF16)
    e_hi = jnp.where(d == fi + half, 1.0, 0.0).astype(BF16)

    def expand(table_t, sel):
        p0 = table_t.astype(BF16).astype(F32)
        r1 = table_t - p0
        p1 = r1.astype(BF16).astype(F32)
        p2 = (r1 - p1).astype(BF16).astype(F32)
        pieces = jnp.concatenate([p0, p1, p2], axis=0).astype(BF16)
        out = lax.dot_general(pieces, sel, (((0,), (0,)), ((), ())), preferred_element_type=F32)
        return jnp.concatenate([out] * (DIFF_QK // LANE), axis=1)

    lane = lax.broadcasted_iota(jnp.int32, (1, DIFF_QK), 1)
    passthrough = jnp.where(jnp.bitwise_and(lane, DIFF_DH - 1) >= ROT_DIM, 1.0, 0.0).astype(F32)
    cos_f = expand(cos_t, e_lo + e_hi) + passthrough
    sin_a = -expand(sin_t, e_lo)
    sin_b = expand(sin_t, e_hi)

    ri = lax.broadcasted_iota(jnp.int32, (DIFF_QK, DIFF_QK), 0)
    ci = lax.broadcasted_iota(jnp.int32, (DIFF_QK, DIFF_QK), 1)
    grp = jnp.where(jnp.right_shift(ri, 6) == jnp.right_shift(ci, 6), 1.0 / DIFF_DH, 0.0).astype(BF16)

    def norm_rot(x, gain):
        sq = x * x
        hi, lo = _split_bf16(sq)
        ms = _dot(hi, grp) + _dot(lo, grp)
        y = x * lax.rsqrt(ms + EPS) * gain
        up = pltpu.roll(y, DIFF_QK - half, 1)
        dn = pltpu.roll(y, half, 1)
        return y * cos_f + up * sin_a + dn * sin_b

    oq_ref[0] = norm_rot(q_ref[...], qg_ref[...] * (DIFF_DH ** -0.5 * LOG2E)).T.astype(BF16)
    ok_ref[...] = norm_rot(k_ref[...], kg_ref[...]).astype(BF16)
    ov_ref[0] = v_ref[...].T.astype(BF16)


def _diff_prep(proj, pos3, freq, qg, kg, tm):
    m = proj.shape[0]
    blk = lambda c: pl.BlockSpec((tm, DIFF_QK), lambda i: (i, c))
    small = lambda s: pl.BlockSpec(s, lambda i: (0,) * len(s))
    out_t = jax.ShapeDtypeStruct((m // tm, DIFF_QK, tm), BF16)
    spec_t = pl.BlockSpec((1, DIFF_QK, tm), lambda i: (i, 0, 0))
    return pl.pallas_call(
        _diffprep_kernel,
        grid=(m // tm,),
        in_specs=[
            pl.BlockSpec((1, 1, tm), lambda i: (i, 0, 0)),
            small((ROT_DIM // 2, 1)),
            blk(COL_DQ // DIFF_QK), blk(COL_DK // DIFF_QK), blk(COL_DV // DIFF_QK),
            small((1, DIFF_QK)), small((1, DIFF_QK)),
        ],
        out_specs=[spec_t, pl.BlockSpec((tm, DIFF_QK), lambda i: (i, 0)), spec_t],
        out_shape=[out_t, jax.ShapeDtypeStruct((m, DIFF_QK), BF16), out_t],
        compiler_params=_cparams(("parallel",)),
        name="diff_prep",
    )(pos3, freq, proj, proj, proj, qg, kg)


def _flash_kernel(qt_ref, k_ref, vt_ref, lq1_ref, lk1_ref, lq2_ref, lk2_ref, gain_ref, o_ref,
                  *scratch, tq, tk, lam_init):
    qi = pl.program_id(2)
    chains = [(hh, c) for hh in range(FLASH_HPB) for c in range(2)]
    acc_scr, s_scr, p_scr = (scratch[k * len(chains):(k + 1) * len(chains)] for k in range(3))
    qst = {}
    for hh in range(FLASH_HPB):
        qt = qt_ref[0, hh * LANE:(hh + 1) * LANE, :]
        row = lax.broadcasted_iota(jnp.int32, qt.shape, 0)
        zero = jnp.zeros_like(qt)
        qst[hh, 0] = jnp.where(row < DIFF_DH, qt, zero)
        qst[hh, 1] = jnp.where(row >= DIFF_DH, qt, zero)

    nch = len(chains)
    last = nch - 1
    for c in range(nch):
        acc_scr[c][...] = jnp.zeros((ACC_ROWS, tq), F32)
    ones_rows = jnp.ones((ACC_ROWS - DIFF_DV, tk), BF16)

    def qk(c, j):
        hh, comp = chains[c]
        start = pl.multiple_of(j * tk, tk)
        s = _dot(k_ref[pl.ds(start, tk), hh * LANE:(hh + 1) * LANE], qst[hh, comp])
        s_scr[c][...] = s
        return jnp.max(s, axis=0, keepdims=True)

    def pv(c, j, alpha):
        hh, comp = chains[c]
        vt_aug = jnp.concatenate([vt_ref[j, hh * LANE:(hh + 1) * LANE, :], ones_rows], axis=0)
        upd = _dot(vt_aug, p_scr[c][...])
        acc_scr[c][...] = alpha * acc_scr[c][...] + upd

    def softmax(c, j, m_old, cmax, masked):
        s = s_scr[c][...]
        if masked:
            r = lax.broadcasted_iota(jnp.int32, (tk, tq), 0)
            col = lax.broadcasted_iota(jnp.int32, (tk, tq), 1)
            s = jnp.where(j * tk + r <= qi * tq + col, s, -jnp.inf)
            cmax = jnp.max(s, axis=0, keepdims=True)
        m_new = jnp.maximum(m_old, cmax)
        alpha = jnp.exp2(m_old - m_new)
        p_scr[c][...] = jnp.exp2(s - m_new).astype(BF16)
        return m_new, alpha

    n_full = (qi * tq) // tk
    cmax0 = tuple(qk(c, 0) for c in range(last))
    p_scr[last][...] = jnp.zeros((tk, tq), BF16)

    def trip(i, carry):
        stats, cmaxs, alpha_last = carry
        stats, cmaxs = list(stats), list(cmaxs)
        cmax_last = qk(last, i)
        pv(last, jnp.maximum(i - 1, 0), alpha_last)
        for c in range(nch):
            stats[c], alpha = softmax(c, i, stats[c], cmaxs[c] if c < last else cmax_last, False)
            if c < last:
                cmaxs[c] = qk(c, i + 1)
                pv(c, i, alpha)
        return tuple(stats), tuple(cmaxs), alpha

    init = (tuple(jnp.full((1, tq), -jnp.inf, F32) for _ in chains), cmax0, jnp.ones((1, tq), F32))
    stats, _, alpha_last = lax.fori_loop(0, n_full, trip, init)
    stats = list(stats)
    qk(last, n_full)
    pv(last, jnp.maximum(n_full - 1, 0), alpha_last)
    for c in range(nch):
        stats[c], alpha = softmax(c, n_full, stats[c], None, True)
        pv(c, n_full, alpha)

    lam = (jnp.exp(jnp.sum(lq1_ref[...] * lk1_ref[...], axis=-1, keepdims=True))
           - jnp.exp(jnp.sum(lq2_ref[...] * lk2_ref[...], axis=-1, keepdims=True)) + lam_init)
    for hh in range(FLASH_HPB):
        a1 = acc_scr[2 * hh][...]
        a2 = acc_scr[2 * hh + 1][...]
        o1 = a1[:DIFF_DV] / a1[DIFF_DV:DIFF_DV + 1]
        o2 = a2[:DIFF_DV] / a2[DIFF_DV:DIFF_DV + 1]
        o = (o1 - lam * o2).T
        ms = jnp.mean(o * o, axis=-1, keepdims=True)
        o_ref[:, hh * LANE:(hh + 1) * LANE] = (
            o * lax.rsqrt(ms + EPS) * gain_ref[...] * (1.0 - lam_init)).astype(o_ref.dtype)


def _flash_diff(dqt, dk, dvt, lq1, lk1, lq2, lk2, gain, b, t, tq, tk, lam_init):
    nq = t // tq
    nkb = t // tk
    qpb = tk // tq
    hw = FLASH_HPB * LANE
    small = lambda s: pl.BlockSpec(s, lambda bi, h, qi: (0,) * len(s))
    return pl.pallas_call(
        functools.partial(_flash_kernel, tq=tq, tk=tk, lam_init=lam_init),
        grid=(b, DIFF_HEADS // FLASH_HPB, nq),
        in_specs=[
            pl.BlockSpec((1, hw, tq), lambda bi, h, qi: (bi * nkb + qi // qpb, h, qi % qpb)),
            pl.BlockSpec((t, hw), lambda bi, h, qi: (bi, h)),
            pl.BlockSpec((nkb, hw, tk), lambda bi, h, qi: (bi, h, 0)),
            small((1, DIFF_DH)), small((1, DIFF_DH)), small((1, DIFF_DH)), small((1, DIFF_DH)),
            small((1, DIFF_DV)),
        ],
        out_specs=pl.BlockSpec((tq, hw), lambda bi, h, qi: (bi * nq + qi, h)),
        out_shape=jax.ShapeDtypeStruct((b * t, DIFF_V), BF16),
        scratch_shapes=([pltpu.VMEM((ACC_ROWS, tq), F32)] * (2 * FLASH_HPB)
                        + [pltpu.VMEM((tk, tq), F32)] * (2 * FLASH_HPB)
                        + [pltpu.VMEM((tk, tq), BF16)] * (2 * FLASH_HPB)),
        compiler_params=_cparams(("parallel", "parallel", "arbitrary")),
        name="flash_diff",
    )(dqt, dk, dvt, lq1, lk1, lq2, lk2, gain)


def _gdnprep_kernel(cur_ref, halo_ref, ab_ref, convw_ref, alog_ref, dtb_ref, alogc_ref, dtbc_ref,
                    wq_ref, u_ref, aq_ref, kdt_ref, eg_ref, *, nblk):
    i = pl.program_id(0)
    n = GDN_BLK
    keep = jnp.where(i % nblk == 0, 0.0, 1.0).astype(F32)
    xs = jnp.concatenate([halo_ref[...] * keep, cur_ref[...]], axis=0)
    cw = convw_ref[...]
    conv = xs[8:8 + n] * cw[CONV_K - 1:CONV_K]
    for j in range(CONV_K - 1):
        off = 8 - (CONV_K - 1) + j
        conv = conv + xs[off:off + n] * cw[j:j + 1]
    qkv = _silu(conv)

    ab = ab_ref[...]
    ab_t = ab.T
    g_col = -jnp.exp(alog_ref[...]) * _softplus(ab + dtb_ref[...])
    beta_col = _sigmoid(ab)
    g_row = -jnp.exp(alogc_ref[...]) * _softplus(ab_t[:8] + dtbc_ref[...])

    sb_n = LANE
    ri = lax.broadcasted_iota(jnp.int32, (sb_n, sb_n), 0)
    ci = lax.broadcasted_iota(jnp.int32, (sb_n, sb_n), 1)
    same = jnp.right_shift(ri, 6) == jnp.right_shift(ci, 6)
    same32 = jnp.right_shift(ri, 5) == jnp.right_shift(ci, 5)
    same16 = jnp.right_shift(ri, 4) == jnp.right_shift(ci, 4)
    incl = jnp.logical_and(same, ri >= ci)
    strict = jnp.logical_and(same, ri > ci)
    one = jnp.ones((sb_n, sb_n), F32)
    zero = jnp.zeros((sb_n, sb_n), F32)
    tri = jnp.where(incl, one, zero).astype(BF16)
    blk_ones = jnp.where(same, one, zero).astype(BF16)
    eye = jnp.where(ri == ci, one, zero)

    def split3(a):
        p0 = a.astype(BF16)
        r1 = a - p0.astype(F32)
        p1 = r1.astype(BF16)
        return p0, p1, (r1 - p1.astype(F32)).astype(BF16)

    units = [(h, sb) for sb in range(n // sb_n) for h in range(GDN_HEADS)]
    gc_cols, gl_cols, gc_rows = [], [], []
    for sb in range(n // sb_n):
        rows = slice(sb * sb_n, (sb + 1) * sb_n)
        pc = split3(g_col[rows])
        pr = split3(g_row[:, rows])
        gc_cols.append(sum(_dot(tri, p) for p in pc))
        gl_cols.append(sum(_dot(blk_ones, p) for p in pc))
        gc_rows.append(sum(_dot_nt(p, tri) for p in pr))

    qs, ks, kbs, kbfs, decays, egcs, glbs, gcbs, rhss = [], [], [], [], [], [], [], [], []
    for h, sb in units:
        rows = slice(sb * sb_n, (sb + 1) * sb_n)
        q = qkv[rows, h * GDN_DK:(h + 1) * GDN_DK]
        k = qkv[rows, GDN_QK + h * GDN_DK: GDN_QK + (h + 1) * GDN_DK]
        v = qkv[rows, 2 * GDN_QK + h * GDN_DV: 2 * GDN_QK + (h + 1) * GDN_DV]
        q = q * lax.rsqrt(jnp.sum(q * q, axis=-1, keepdims=True) + EPS) * (GDN_DK ** -0.5)
        k = k * lax.rsqrt(jnp.sum(k * k, axis=-1, keepdims=True) + EPS)
        beta = beta_col[rows, GDN_HEADS + h: GDN_HEADS + h + 1]
        gcb = jnp.broadcast_to(gc_cols[sb][:, h:h + 1], (sb_n, GDN_DK))
        glb = jnp.broadcast_to(gl_cols[sb][:, h:h + 1], (sb_n, GDN_DK))
        gdiff = gcb - gc_rows[sb][h:h + 1, :]
        kb = k * beta
        egc = jnp.exp(gcb)
        qs.append(q); ks.append(k); kbs.append(kb); kbfs.append(k.astype(BF16))
        decays.append(jnp.where(incl, jnp.exp(jnp.where(incl, gdiff, zero)), zero))
        egcs.append(egc); glbs.append(glb); gcbs.append(gcb)
        rhss.append(jnp.concatenate([v * beta, kb * egc], axis=1).astype(BF16))

    nu = range(len(units))
    kq = [_dot_nt(jnp.concatenate([kbs[i], qs[i]], axis=0).astype(BF16), kbfs[i]) for i in nu]
    lowers = [jnp.where(strict, kq[i][:sb_n] * decays[i], zero) for i in nu]
    xs_ = [jnp.where(same16, -lowers[i], zero) for i in nu]
    tmats = [eye + xs_[i] for i in nu]
    for _ in range(int(math.log2(INV_BASE)) - 1):
        xs_ = [_tdot(xs_[i], xs_[i]) for i in nu]
        tmats = [tmats[i] + _tdot(tmats[i], xs_[i]) for i in nu]
    for lo_mask, hi_mask in ((same16, same32), (same32, same)):
        sel = jnp.logical_and(hi_mask, jnp.logical_not(lo_mask))
        tmp = [_tdot(tmats[i], jnp.where(sel, lowers[i], zero)) for i in nu]
        tmats = [tmats[i] - _tdot(tmp[i], tmats[i]) for i in nu]
    uws = [_dot(tmats[i].astype(BF16), rhss[i]) for i in nu]

    cps = sb_n // CHUNK
    for i, (h, sb) in enumerate(units):
        u = uws[i][:, :GDN_DV]
        w = uws[i][:, GDN_DV:].astype(BF16)
        a_intra = (kq[i][sb_n:] * decays[i]).astype(BF16)
        q_dec = (qs[i] * egcs[i]).astype(BF16)
        kd_t = (ks[i] * jnp.exp(glbs[i] - gcbs[i])).T.astype(BF16)
        eg = jnp.exp(glbs[i])
        for cc in range(cps):
            c = sb * cps + cc
            rs = slice(cc * CHUNK, (cc + 1) * CHUNK)
            wq_ref[0, h, c, 0:CHUNK, :] = w[rs]
            wq_ref[0, h, c, CHUNK:2 * CHUNK, :] = q_dec[rs]
            u_ref[0, h, c] = u[rs]
            aq_ref[0, h, c] = a_intra[rs, rs]
            kdt_ref[0, h, c] = kd_t[:, rs]
            eg_ref[0, h, c] = eg[cc * CHUNK: cc * CHUNK + 1, :]


def _gdn_prep(proj, conv_w, a_log_pat, dtb_pat, a_log_col, dtb_col, b, t):
    nblk = t // GDN_BLK
    nc = t // CHUNK
    c3 = 2 * GDN_QK + GDN_V
    small = lambda s: pl.BlockSpec(s, lambda i: (0,) * len(s))
    omap = lambda i: (i // nblk, 0, i % nblk, 0, 0)
    return pl.pallas_call(
        functools.partial(_gdnprep_kernel, nblk=nblk),
        grid=(b * nblk,),
        in_specs=[
            pl.BlockSpec((GDN_BLK, c3), lambda i: (i, COL_QKV // c3)),
            pl.BlockSpec((8, c3), lambda i: (jnp.maximum(i * (GDN_BLK // 8) - 1, 0), COL_QKV // c3)),
            pl.BlockSpec((GDN_BLK, LANE), lambda i: (i, COL_AB // LANE)),
            small((CONV_K, c3)), small((1, LANE)), small((1, LANE)), small((8, 1)), small((8, 1)),
        ],
        out_specs=[
            pl.BlockSpec((1, GDN_HEADS, CPB, 2 * CHUNK, GDN_DK), omap),
            pl.BlockSpec((1, GDN_HEADS, CPB, CHUNK, GDN_DV), omap),
            pl.BlockSpec((1, GDN_HEADS, CPB, CHUNK, CHUNK), omap),
            pl.BlockSpec((1, GDN_HEADS, CPB, GDN_DK, CHUNK), omap),
            pl.BlockSpec((1, GDN_HEADS, CPB, 1, GDN_DV), omap),
        ],
        out_shape=[
            jax.ShapeDtypeStruct((b, GDN_HEADS, nc, 2 * CHUNK, GDN_DK), BF16),
            jax.ShapeDtypeStruct((b, GDN_HEADS, nc, CHUNK, GDN_DV), F32),
            jax.ShapeDtypeStruct((b, GDN_HEADS, nc, CHUNK, CHUNK), BF16),
            jax.ShapeDtypeStruct((b, GDN_HEADS, nc, GDN_DK, CHUNK), BF16),
            jax.ShapeDtypeStruct((b, GDN_HEADS, nc, 1, GDN_DV), F32),
        ],
        compiler_params=_cparams(("parallel",)),
        name="gdn_prep",
    )(proj, proj, proj, conv_w, a_log_pat, dtb_pat, a_log_col, dtb_col)


def _gdnscan_kernel(wq_ref, u_ref, aq_ref, kdt_ref, eg_ref, o_ref, s_scr, *, nb, cb):
    @pl.when(pl.program_id(0) == 0)
    def _():
        s_scr[...] = jnp.zeros(s_scr.shape, F32)

    streams = [(bi, h) for bi in range(nb) for h in range(GDN_HEADS)]

    def chunk(c, carry):
        rs = [_dot(wq_ref[bi, h, c], s_scr[bi, h].astype(BF16)) for bi, h in streams]
        vbs = [(u_ref[bi, h, c] - r[:CHUNK]).astype(BF16) for (bi, h), r in zip(streams, rs)]
        rows = pl.ds(pl.multiple_of(c * CHUNK, CHUNK), CHUNK)
        for (bi, h), r, vb in zip(streams, rs, vbs):
            o_ref[bi, h, rows, :] = r[CHUNK:] + _dot(aq_ref[bi, h, c], vb)
        for (bi, h), vb in zip(streams, vbs):
            s_scr[bi, h] = s_scr[bi, h] * eg_ref[bi, h, c] + _dot(kdt_ref[bi, h, c], vb)
        return carry

    lax.fori_loop(0, cb, chunk, 0)


def _gdn_scan(wq, u, aq, kdt, eg, cb):
    b, _, nc = wq.shape[:3]
    spec = lambda d0, d1: pl.BlockSpec((b, GDN_HEADS, cb, d0, d1), lambda i: (0, 0, i, 0, 0))
    return pl.pallas_call(
        functools.partial(_gdnscan_kernel, nb=b, cb=cb),
        grid=(nc // cb,),
        in_specs=[spec(2 * CHUNK, GDN_DK), spec(CHUNK, GDN_DV), spec(CHUNK, CHUNK),
                  spec(GDN_DK, CHUNK), spec(1, GDN_DV)],
        out_specs=pl.BlockSpec((b, GDN_HEADS, cb * CHUNK, GDN_DV), lambda i: (0, 0, i, 0)),
        out_shape=jax.ShapeDtypeStruct((b, GDN_HEADS, nc * CHUNK, GDN_DV), F32),
        scratch_shapes=[pltpu.VMEM((b, GDN_HEADS, GDN_DK, GDN_DV), F32)],
        compiler_params=_cparams(("arbitrary",)),
        name="gdn_scan",
    )(wq, u, aq, kdt, eg)


def _merge_kernel(x_ref, an_ref, wgz_ref, oa_ref, ob_ref, gn_ref, woa_ref, wob_ref, wout_ref, o_ref):
    x = x_ref[...]
    ms = jnp.mean(x * x, axis=-1, keepdims=True)
    hn = (x * lax.rsqrt(ms + EPS) * an_ref[...]).astype(BF16)
    gz = _dot(hn, wgz_ref[...])
    parts = []
    for h in range(GDN_HEADS):
        o = oa_ref[0, h]
        ms = jnp.mean(o * o, axis=-1, keepdims=True)
        z = gz[:, h * GDN_DV:(h + 1) * GDN_DV]
        parts.append((o * lax.rsqrt(ms + EPS) * gn_ref[...] * _silu(z)).astype(BF16))
    y_a = _dot(jnp.concatenate(parts, axis=1), woa_ref[...])
    y_b = _dot(ob_ref[...], wob_ref[...])
    merged = (_sigmoid(gz[:, GDN_V:GDN_V + D_MODEL]) * y_a + _sigmoid(gz[:, GDN_V + D_MODEL:]) * y_b)
    o_ref[...] = x + _dot(merged.astype(BF16), wout_ref[...])


def _merge(x2d, attn_gain, w_gz, o_a, o_b, gdn_gain, w_o_a, w_o_b, w_out, t, tm):
    m = x2d.shape[0]
    npb = t // tm
    small = lambda s: pl.BlockSpec(s, lambda i: (0,) * len(s), pipeline_mode=pl.Buffered(1))
    return pl.pallas_call(
        _merge_kernel,
        grid=(m // tm,),
        in_specs=[
            pl.BlockSpec((tm, D_MODEL), lambda i: (i, 0)),
            small((1, D_MODEL)), small((D_MODEL, N_GZ)),
            pl.BlockSpec((1, GDN_HEADS, tm, GDN_DV), lambda i: (i // npb, 0, i % npb, 0)),
            pl.BlockSpec((tm, DIFF_V), lambda i: (i, 0)),
            small((1, GDN_DV)), small((GDN_V, D_MODEL)), small((DIFF_V, D_MODEL)), small((D_MODEL, D_MODEL)),
        ],
        out_specs=pl.BlockSpec((tm, D_MODEL), lambda i: (i, 0)),
        out_shape=jax.ShapeDtypeStruct((m, D_MODEL), F32),
        compiler_params=_cparams(("parallel",)),
        name="merge",
    )(x2d, attn_gain, w_gz, o_a, o_b, gdn_gain, w_o_a, w_o_b, w_out)


def _mlp_kernel(x_ref, mg_ref, wup_ref, wdn_ref, pg_ref, wpg_ref, p_ref, wp_ref, o_ref, h_scr, acc_scr):
    f = pl.program_id(1)

    @pl.when(f == 0)
    def _():
        x = x_ref[...]
        ms = jnp.mean(x * x, axis=-1, keepdims=True)
        h_scr[...] = (x * lax.rsqrt(ms + EPS) * mg_ref[...]).astype(BF16)
        acc_scr[...] = jnp.zeros(acc_scr.shape, F32)

    a = jnp.maximum(_dot(h_scr[...], wup_ref[...]), 0.0)
    acc_scr[...] += _dot((a * a).astype(BF16), wdn_ref[...])

    @pl.when(f == pl.num_programs(1) - 1)
    def _():
        x2 = x_ref[...] + acc_scr[...]
        ms = jnp.mean(x2 * x2, axis=-1, keepdims=True)
        hn = (x2 * lax.rsqrt(ms + EPS) * pg_ref[...]).astype(BF16)
        gate = _sigmoid(_dot(hn, wpg_ref[...]))
        o_ref[...] = x2 + gate * _dot(p_ref[0, 0].astype(BF16), wp_ref[...])


def _mlp_ple(x1, mlp_gain, w_up, w_down, ple_gain, w_ple_gate, p, layer, w_ple, tm, tf):
    m = x1.shape[0]
    npb = p.shape[2] // tm
    small = lambda s: pl.BlockSpec(s, lambda i, f: (0,) * len(s))
    return pl.pallas_call(
        _mlp_kernel,
        grid=(m // tm, D_FF // tf),
        in_specs=[
            pl.BlockSpec((tm, D_MODEL), lambda i, f: (i, 0)),
            small((1, D_MODEL)),
            pl.BlockSpec((D_MODEL, tf), lambda i, f: (0, f)),
            pl.BlockSpec((tf, D_MODEL), lambda i, f: (f, 0)),
            small((1, D_MODEL)), small((D_MODEL, D_MODEL)),
            pl.BlockSpec((1, 1, tm, PLE_DIM), lambda i, f: (layer, i // npb, i % npb, 0)),
            small((PLE_DIM, D_MODEL)),
        ],
        out_specs=pl.BlockSpec((tm, D_MODEL), lambda i, f: (i, 0)),
        out_shape=jax.ShapeDtypeStruct((m, D_MODEL), F32),
        scratch_shapes=[pltpu.VMEM((tm, D_MODEL), BF16), pltpu.VMEM((tm, D_MODEL), F32)],
        compiler_params=_cparams(("parallel", "arbitrary")),
        name="mlp_ple",
    )(x1, mlp_gain, w_up, w_down, ple_gain, w_ple_gate, p, w_ple)


def _split_w_in(w):
    o_z = 3 * GDN_QK
    o_a = o_z + GDN_V
    o_dq = o_a + 2 * GDN_HEADS
    o_gate = o_dq + 3 * DIFF_QK
    pad = jnp.zeros((D_MODEL, N_PAD - COL_AB - 2 * GDN_HEADS), w.dtype)
    w_main = jnp.concatenate([w[:, :o_z], w[:, o_dq:o_gate], w[:, o_a:o_dq], pad], axis=1).astype(BF16)
    w_gz = jnp.concatenate([w[:, o_z:o_a], w[:, o_gate:]], axis=1).astype(BF16)
    return w_main, w_gz


def _lane_pat(vec, offset):
    return jnp.zeros((1, LANE), F32).at[0, offset:offset + GDN_HEADS].set(vec.astype(F32))


def _col_pat(vec):
    return jnp.zeros((8, 1), F32).at[:GDN_HEADS, 0].set(vec.astype(F32))


def _pick(n, pref):
    return pref if n % pref == 0 else n


def kernel(x, p, positions, attn_norm, w_in, conv_w, a_log, dt_bias, gdn_norm, w_o_a, q_norm, k_norm,
           lambda_q1, lambda_k1, lambda_q2, lambda_k2, diff_norm, w_o_b, w_out, mlp_norm, w_up, w_down,
           ple_norm, w_ple_gate, w_ple):
    b, t, _ = x.shape
    m = b * t
    depth = w_in.shape[0]
    assert t % GDN_BLK == 0, "sequence length must be a multiple of the gdn_prep block"

    tm_big = _pick(m, 1024)
    tm_mid = _pick(m, 512)
    tq = _pick(t, 512)
    tk = _pick(t, 512)
    assert tk % tq == 0
    cb = _pick(t // CHUNK, 8)
    pos3 = positions.reshape(m // tk, 1, tk)
    freq = (ROPE_THETA ** (-jnp.arange(0, ROT_DIM, 2, dtype=F32) / ROT_DIM)).reshape(ROT_DIM // 2, 1)
    row = lambda v: v.astype(F32).reshape(1, -1)

    x2d = x.reshape(m, D_MODEL)
    for i in range(depth):
        lam_init = 0.8 - 0.6 * math.exp(-0.3 * i)
        w_main, w_gz = _split_w_in(w_in[i])
        proj = _in_proj(x2d, row(attn_norm[i]), w_main, tm_mid, 512)

        dqt, dk, dvt = _diff_prep(proj, pos3, freq, jnp.tile(row(q_norm[i]), (1, DIFF_QK // DIFF_DH)),
                                  jnp.tile(row(k_norm[i]), (1, DIFF_QK // DIFF_DH)), tk)
        o_b = _flash_diff(dqt, dk, dvt, row(lambda_q1[i]), row(lambda_k1[i]), row(lambda_q2[i]),
                          row(lambda_k2[i]), row(diff_norm[i]), b, t, tq, tk, lam_init)

        wq, u, aq, kdt, eg = _gdn_prep(proj, conv_w[i].astype(F32), _lane_pat(a_log[i], 0),
                                       _lane_pat(dt_bias[i], 0), _col_pat(a_log[i]), _col_pat(dt_bias[i]), b, t)
        o_a = _gdn_scan(wq, u, aq, kdt, eg, cb)

        x1 = _merge(x2d, row(attn_norm[i]), w_gz, o_a, o_b, row(gdn_norm[i]), w_o_a[i].astype(BF16),
                    w_o_b[i].astype(BF16), w_out[i].astype(BF16), t, _pick(t, 512))
        x2d = _mlp_ple(x1, row(mlp_norm[i]), w_up[i].astype(BF16), w_down[i].astype(BF16), row(ple_norm[i]),
                       w_ple_gate[i].astype(BF16), p, i, w_ple[i].astype(BF16), _pick(t, 1024), 1024)
    return x2d.reshape(b, t, D_MODEL)
```

```python
import functools
import math

import jax
import jax.numpy as jnp
from jax import lax
from jax.experimental import pallas as pl
from jax.experimental.pallas import tpu as pltpu

F32 = jnp.float32
BF16 = jnp.bfloat16

D_MODEL = 1024
PLE_DIM = 256
EPS = 1e-6
GDN_HEADS = 4
GDN_DK = 128
GDN_DV = 128
CONV_K = 4
CHUNK = 64
DIFF_HEADS = 4
DIFF_DH = 64
DIFF_DV = 2 * DIFF_DH
ROT_DIM = DIFF_DH // 4
ROPE_THETA = 500000.0
D_FF = 4 * D_MODEL

GDN_QK = GDN_HEADS * GDN_DK
GDN_V = GDN_HEADS * GDN_DV
DIFF_QK = DIFF_HEADS * 2 * DIFF_DH
DIFF_V = DIFF_HEADS * DIFF_DV
D_IN = 4 * GDN_QK + 2 * GDN_HEADS + 3 * DIFF_QK + 2 * D_MODEL

COL_QKV = 0
COL_DQ = 1536
COL_DK = 2048
COL_DV = 2560
COL_AB = 3072
N_PAD = COL_AB + 128
N_GZ = GDN_V + 2 * D_MODEL

LANE = 128
GDN_BLK = 256
CPB = GDN_BLK // CHUNK
FLASH_HPB = 4
LOG2E = math.log2(math.e)
ACC_ROWS = DIFF_DV + 16
QK_LAG = 1
PV_LAG = 1
INV_BASE = 16
INV_PASSES = 1
VMEM_LIMIT = 48 * 1024 * 1024
FLASH_VMEM_LIMIT = 58 * 1024 * 1024

HI = lax.Precision.HIGHEST


def _cparams(sem):
    return pltpu.CompilerParams(dimension_semantics=sem, vmem_limit_bytes=VMEM_LIMIT)


def _dot(a, b):
    return jnp.dot(a, b, preferred_element_type=F32)


def _dot_nt(a, b):
    return lax.dot_general(a, b, (((1,), (1,)), ((), ())), preferred_element_type=F32)


def _split_bf16(a):
    hi = a.astype(BF16)
    lo = (a - hi.astype(F32)).astype(BF16)
    return hi, lo


def _dot3(a, b):
    ah, al = _split_bf16(a)
    bh, bl = _split_bf16(b)
    return _dot(ah, bh) + _dot(ah, bl) + _dot(al, bh)


def _tdot(a, b):
    if INV_PASSES == 3:
        return _dot3(a, b)
    return _dot(a.astype(BF16), b.astype(BF16))


def _sigmoid(x):
    return 1.0 / (1.0 + jnp.exp(-x))


def _silu(x):
    return x * _sigmoid(x)


def _softplus(x):
    return jnp.maximum(x, 0.0) + jnp.log(1.0 + jnp.exp(-jnp.abs(x)))


def _inproj_kernel(x_ref, g_ref, w_ref, o_ref, *, tn):
    x = x_ref[...]
    ms = jnp.mean(x * x, axis=-1, keepdims=True)
    h = (x * lax.rsqrt(ms + EPS) * g_ref[...]).astype(BF16)
    for lo in range(0, N_PAD, tn):
        hi = min(lo + tn, N_PAD)
        o_ref[:, lo:hi] = _dot(h, w_ref[:, lo:hi])


def _in_proj(x2d, gain, w_bf16, tm, tn):
    m = x2d.shape[0]
    return pl.pallas_call(
        functools.partial(_inproj_kernel, tn=tn),
        grid=(m // tm,),
        in_specs=[
            pl.BlockSpec((tm, D_MODEL), lambda i: (i, 0)),
            pl.BlockSpec((1, D_MODEL), lambda i: (0, 0)),
            pl.BlockSpec((D_MODEL, N_PAD), lambda i: (0, 0), pipeline_mode=pl.Buffered(1)),
        ],
        out_specs=pl.BlockSpec((tm, N_PAD), lambda i: (i, 0)),
        out_shape=jax.ShapeDtypeStruct((m, N_PAD), F32),
        compiler_params=_cparams(("parallel",)),
        name="in_proj",
    )(x2d, gain, w_bf16)


def _diffprep_kernel(pos_ref, freq_ref, q_ref, k_ref, v_ref, qg_ref, kg_ref, oq_ref, ok_ref, ov_ref):
    tm = q_ref.shape[0]
    ang = freq_ref[...] * pos_ref[0].astype(F32)
    cos_t, sin_t = jnp.cos(ang), jnp.sin(ang)
    half = ROT_DIM // 2
    fi = jnp.bitwise_and(lax.broadcasted_iota(jnp.int32, (3 * half, LANE), 0), half - 1)
    d = jnp.bitwise_and(lax.broadcasted_iota(jnp.int32, (3 * half, LANE), 1), DIFF_DH - 1)
    e_lo = jnp.where(d == fi, 1.0, 0.0).astype(BF16)
    e_hi = jnp.where(d == fi + half, 1.0, 0.0).astype(BF16)

    def expand(table_t, sel):
        p0 = table_t.astype(BF16).astype(F32)
        r1 = table_t - p0
        p1 = r1.astype(BF16).astype(F32)
        p2 = (r1 - p1).astype(BF16).astype(F32)
        pieces = jnp.concatenate([p0, p1, p2], axis=0).astype(BF16)
        out = lax.dot_general(pieces, sel, (((0,), (0,)), ((), ())), preferred_element_type=F32)
        return jnp.concatenate([out] * (DIFF_QK // LANE), axis=1)

    lane = lax.broadcasted_iota(jnp.int32, (1, DIFF_QK), 1)
    passthrough = jnp.where(jnp.bitwise_and(lane, DIFF_DH - 1) >= ROT_DIM, 1.0, 0.0).astype(F32)
    cos_f = expand(cos_t, e_lo + e_hi) + passthrough
    sin_a = -expand(sin_t, e_lo)
    sin_b = expand(sin_t, e_hi)

    ri = lax.broadcasted_iota(jnp.int32, (DIFF_QK, DIFF_QK), 0)
    ci = lax.broadcasted_iota(jnp.int32, (DIFF_QK, DIFF_QK), 1)
    grp = jnp.where(jnp.right_shift(ri, 6) == jnp.right_shift(ci, 6), 1.0 / DIFF_DH, 0.0).astype(BF16)

    def norm_rot(x, gain):
        sq = x * x
        hi, lo = _split_bf16(sq)
        ms = _dot(hi, grp) + _dot(lo, grp)
        y = x * lax.rsqrt(ms + EPS) * gain
        up = pltpu.roll(y, DIFF_QK - half, 1)
        dn = pltpu.roll(y, half, 1)
        return y * cos_f + up * sin_a + dn * sin_b

    oq_ref[0] = norm_rot(q_ref[...], qg_ref[...] * (DIFF_DH ** -0.5 * LOG2E)).T.astype(BF16)
    ok_ref[...] = norm_rot(k_ref[...], kg_ref[...]).astype(BF16)
    ov_ref[0] = v_ref[...].T.astype(BF16)


def _diff_prep(proj, pos3, freq, qg, kg, tm):
    m = proj.shape[0]
    blk = lambda c: pl.BlockSpec((tm, DIFF_QK), lambda i: (i, c))
    small = lambda s: pl.BlockSpec(s, lambda i: (0,) * len(s))
    out_t = jax.ShapeDtypeStruct((m // tm, DIFF_QK, tm), BF16)
    spec_t = pl.BlockSpec((1, DIFF_QK, tm), lambda i: (i, 0, 0))
    return pl.pallas_call(
        _diffprep_kernel,
        grid=(m // tm,),
        in_specs=[
            pl.BlockSpec((1, 1, tm), lambda i: (i, 0, 0)),
            small((ROT_DIM // 2, 1)),
            blk(COL_DQ // DIFF_QK), blk(COL_DK // DIFF_QK), blk(COL_DV // DIFF_QK),
            small((1, DIFF_QK)), small((1, DIFF_QK)),
        ],
        out_specs=[spec_t, pl.BlockSpec((tm, DIFF_QK), lambda i: (i, 0)), spec_t],
        out_shape=[out_t, jax.ShapeDtypeStruct((m, DIFF_QK), BF16), out_t],
        compiler_params=_cparams(("parallel",)),
        name="diff_prep",
    )(pos3, freq, proj, proj, proj, qg, kg)


def _flash_kernel(qt_ref, k_ref, vt_ref, lq1_ref, lk1_ref, lq2_ref, lk2_ref, gain_ref, o_ref,
                  *scratch, tq, tk, lam_init):
    qi = pl.program_id(2)
    chains = [(hh, c) for hh in range(FLASH_HPB) for c in range(2)]
    acc_scr, s_scr, p_scr = (scratch[k * len(chains):(k + 1) * len(chains)] for k in range(3))
    qst = {}
    for hh in range(FLASH_HPB):
        qt = qt_ref[0, hh * LANE:(hh + 1) * LANE, :]
        row = lax.broadcasted_iota(jnp.int32, qt.shape, 0)
        zero = jnp.zeros_like(qt)
        qst[hh, 0] = jnp.where(row < DIFF_DH, qt, zero)
        qst[hh, 1] = jnp.where(row >= DIFF_DH, qt, zero)

    nch = len(chains)
    last = nch - 1
    for c in range(nch):
        acc_scr[c][...] = jnp.zeros((ACC_ROWS, tq), F32)
    ones_rows = jnp.ones((ACC_ROWS - DIFF_DV, tk), BF16)

    def qk(c, j):
        hh, comp = chains[c]
        start = pl.multiple_of(j * tk, tk)
        s = _dot(k_ref[pl.ds(start, tk), hh * LANE:(hh + 1) * LANE], qst[hh, comp])
        s_scr[c][...] = s
        return jnp.max(s, axis=0, keepdims=True)

    def pv(c, j, alpha):
        hh, comp = chains[c]
        vt_aug = jnp.concatenate([vt_ref[j, hh * LANE:(hh + 1) * LANE, :], ones_rows], axis=0)
        upd = _dot(vt_aug, p_scr[c][...])
        acc_scr[c][...] = alpha * acc_scr[c][...] + upd

    def softmax(c, j, m_old, cmax, masked):
        s = s_scr[c][...]
        if masked:
            r = lax.broadcasted_iota(jnp.int32, (tk, tq), 0)
            col = lax.broadcasted_iota(jnp.int32, (tk, tq), 1)
            s = jnp.where(j * tk + r <= qi * tq + col, s, -jnp.inf)
            cmax = jnp.max(s, axis=0, keepdims=True)
        m_new = jnp.maximum(m_old, cmax)
        alpha = jnp.exp2(m_old - m_new)
        p_scr[c][...] = jnp.exp2(s - m_new).astype(BF16)
        return m_new, alpha

    n_full = (qi * tq) // tk
    n_early = nch - QK_LAG
    cmax0 = tuple(qk(c, 0) for c in range(n_early))
    for c in range(nch - PV_LAG, nch):
        p_scr[c][...] = jnp.zeros((tk, tq), BF16)

    def trip(i, carry):
        stats, cmaxs, alpha_tail = carry
        stats, cmaxs = list(stats), list(cmaxs)
        cmaxs += [qk(c, i) for c in range(n_early, nch)]
        for k, c in enumerate(range(nch - PV_LAG, nch)):
            pv(c, jnp.maximum(i - 1, 0), alpha_tail[k])
        alphas = [None] * nch
        for c in range(nch):
            stats[c], alphas[c] = softmax(c, i, stats[c], cmaxs[c], False)
            if c >= QK_LAG:
                cmaxs[c - QK_LAG] = qk(c - QK_LAG, i + 1)
            if c >= PV_LAG:
                pv(c - PV_LAG, i, alphas[c - PV_LAG])
        return tuple(stats), tuple(cmaxs[:n_early]), tuple(alphas[nch - PV_LAG:])

    init = (tuple(jnp.full((1, tq), -jnp.inf, F32) for _ in chains), cmax0,
            tuple(jnp.ones((1, tq), F32) for _ in range(PV_LAG)))
    stats, _, alpha_tail = lax.fori_loop(0, n_full, trip, init)
    stats = list(stats)
    for c in range(n_early, nch):
        qk(c, n_full)
    for k, c in enumerate(range(nch - PV_LAG, nch)):
        pv(c, jnp.maximum(n_full - 1, 0), alpha_tail[k])
    for c in range(nch):
        stats[c], alpha = softmax(c, n_full, stats[c], None, True)
        pv(c, n_full, alpha)

    lam = (jnp.exp(jnp.sum(lq1_ref[...] * lk1_ref[...], axis=-1, keepdims=True))
           - jnp.exp(jnp.sum(lq2_ref[...] * lk2_ref[...], axis=-1, keepdims=True)) + lam_init)
    for hh in range(FLASH_HPB):
        a1 = acc_scr[2 * hh][...]
        a2 = acc_scr[2 * hh + 1][...]
        o1 = a1[:DIFF_DV] / a1[DIFF_DV:DIFF_DV + 1]
        o2 = a2[:DIFF_DV] / a2[DIFF_DV:DIFF_DV + 1]
        o = (o1 - lam * o2).T
        ms = jnp.mean(o * o, axis=-1, keepdims=True)
        o_ref[:, hh * LANE:(hh + 1) * LANE] = (
            o * lax.rsqrt(ms + EPS) * gain_ref[...] * (1.0 - lam_init)).astype(o_ref.dtype)


def _flash_diff(dqt, dk, dvt, lq1, lk1, lq2, lk2, gain, b, t, tq, tk, lam_init):
    nq = t // tq
    nkb = t // tk
    qpb = tk // tq
    hw = FLASH_HPB * LANE
    small = lambda s: pl.BlockSpec(s, lambda bi, h, qi: (0,) * len(s))
    return pl.pallas_call(
        functools.partial(_flash_kernel, tq=tq, tk=tk, lam_init=lam_init),
        grid=(b, DIFF_HEADS // FLASH_HPB, nq),
        in_specs=[
            pl.BlockSpec((1, hw, tq), lambda bi, h, qi: (bi * nkb + qi // qpb, h, qi % qpb)),
            pl.BlockSpec((t, hw), lambda bi, h, qi: (bi, h)),
            pl.BlockSpec((nkb, hw, tk), lambda bi, h, qi: (bi, h, 0)),
            small((1, DIFF_DH)), small((1, DIFF_DH)), small((1, DIFF_DH)), small((1, DIFF_DH)),
            small((1, DIFF_DV)),
        ],
        out_specs=pl.BlockSpec((tq, hw), lambda bi, h, qi: (bi * nq + qi, h)),
        out_shape=jax.ShapeDtypeStruct((b * t, DIFF_V), BF16),
        scratch_shapes=([pltpu.VMEM((ACC_ROWS, tq), F32)] * (2 * FLASH_HPB)
                        + [pltpu.VMEM((tk, tq), F32)] * (2 * FLASH_HPB)
                        + [pltpu.VMEM((tk, tq), BF16)] * (2 * FLASH_HPB)),
        compiler_params=pltpu.CompilerParams(dimension_semantics=("parallel", "parallel", "arbitrary"),
                                             vmem_limit_bytes=FLASH_VMEM_LIMIT),
        name="flash_diff",
    )(dqt, dk, dvt, lq1, lk1, lq2, lk2, gain)


def _gdnprep_kernel(cur_ref, halo_ref, ab_ref, convw_ref, alog_ref, dtb_ref, alogc_ref, dtbc_ref,
                    wq_ref, u_ref, aq_ref, kdt_ref, eg_ref, *, nblk):
    i = pl.program_id(0)
    n = GDN_BLK
    keep = jnp.where(i % nblk == 0, 0.0, 1.0).astype(F32)
    xs = jnp.concatenate([halo_ref[...] * keep, cur_ref[...]], axis=0)
    cw = convw_ref[...]
    conv = xs[8:8 + n] * cw[CONV_K - 1:CONV_K]
    for j in range(CONV_K - 1):
        off = 8 - (CONV_K - 1) + j
        conv = conv + xs[off:off + n] * cw[j:j + 1]
    qkv = _silu(conv)

    ab = ab_ref[...]
    ab_t = ab.T
    g_col = -jnp.exp(alog_ref[...]) * _softplus(ab + dtb_ref[...])
    beta_col = _sigmoid(ab)
    g_row = -jnp.exp(alogc_ref[...]) * _softplus(ab_t[:8] + dtbc_ref[...])

    sb_n = LANE
    ri = lax.broadcasted_iota(jnp.int32, (sb_n, sb_n), 0)
    ci = lax.broadcasted_iota(jnp.int32, (sb_n, sb_n), 1)
    same = jnp.right_shift(ri, 6) == jnp.right_shift(ci, 6)
    same32 = jnp.right_shift(ri, 5) == jnp.right_shift(ci, 5)
    same16 = jnp.right_shift(ri, 4) == jnp.right_shift(ci, 4)
    incl = jnp.logical_and(same, ri >= ci)
    strict = jnp.logical_and(same, ri > ci)
    one = jnp.ones((sb_n, sb_n), F32)
    zero = jnp.zeros((sb_n, sb_n), F32)
    tri = jnp.where(incl, one, zero).astype(BF16)
    blk_ones = jnp.where(same, one, zero).astype(BF16)
    eye = jnp.where(ri == ci, one, zero)

    def split3(a):
        p0 = a.astype(BF16)
        r1 = a - p0.astype(F32)
        p1 = r1.astype(BF16)
        return p0, p1, (r1 - p1.astype(F32)).astype(BF16)

    units = [(h, sb) for sb in range(n // sb_n) for h in range(GDN_HEADS)]
    gc_cols, gl_cols, gc_rows = [], [], []
    for sb in range(n // sb_n):
        rows = slice(sb * sb_n, (sb + 1) * sb_n)
        pc = split3(g_col[rows])
        pr = split3(g_row[:, rows])
        gc_cols.append(sum(_dot(tri, p) for p in pc))
        gl_cols.append(sum(_dot(blk_ones, p) for p in pc))
        gc_rows.append(sum(_dot_nt(p, tri) for p in pr))

    qs, ks, kbs, kbfs, decays, egcs, glbs, gcbs, rhss = [], [], [], [], [], [], [], [], []
    for h, sb in units:
        rows = slice(sb * sb_n, (sb + 1) * sb_n)
        q = qkv[rows, h * GDN_DK:(h + 1) * GDN_DK]
        k = qkv[rows, GDN_QK + h * GDN_DK: GDN_QK + (h + 1) * GDN_DK]
        v = qkv[rows, 2 * GDN_QK + h * GDN_DV: 2 * GDN_QK + (h + 1) * GDN_DV]
        q = q * lax.rsqrt(jnp.sum(q * q, axis=-1, keepdims=True) + EPS) * (GDN_DK ** -0.5)
        k = k * lax.rsqrt(jnp.sum(k * k, axis=-1, keepdims=True) + EPS)
        beta = beta_col[rows, GDN_HEADS + h: GDN_HEADS + h + 1]
        gcb = jnp.broadcast_to(gc_cols[sb][:, h:h + 1], (sb_n, GDN_DK))
        glb = jnp.broadcast_to(gl_cols[sb][:, h:h + 1], (sb_n, GDN_DK))
        gdiff = gcb - gc_rows[sb][h:h + 1, :]
        kb = k * beta
        egc = jnp.exp(gcb)
        qs.append(q); ks.append(k); kbs.append(kb); kbfs.append(k.astype(BF16))
        decays.append(jnp.where(incl, jnp.exp(jnp.where(incl, gdiff, zero)), zero))
        egcs.append(egc); glbs.append(glb); gcbs.append(gcb)
        rhss.append(jnp.concatenate([v * beta, kb * egc], axis=1).astype(BF16))

    nu = range(len(units))
    kq = [_dot_nt(jnp.concatenate([kbs[i], qs[i]], axis=0).astype(BF16), kbfs[i]) for i in nu]
    lowers = [jnp.where(strict, kq[i][:sb_n] * decays[i], zero) for i in nu]
    xs_ = [jnp.where(same16, -lowers[i], zero) for i in nu]
    tmats = [eye + xs_[i] for i in nu]
    for _ in range(int(math.log2(INV_BASE)) - 1):
        xs_ = [_tdot(xs_[i], xs_[i]) for i in nu]
        tmats = [tmats[i] + _tdot(tmats[i], xs_[i]) for i in nu]
    for lo_mask, hi_mask in ((same16, same32), (same32, same)):
        sel = jnp.logical_and(hi_mask, jnp.logical_not(lo_mask))
        tmp = [_tdot(tmats[i], jnp.where(sel, lowers[i], zero)) for i in nu]
        tmats = [tmats[i] - _tdot(tmp[i], tmats[i]) for i in nu]
    uws = [_dot(tmats[i].astype(BF16), rhss[i]) for i in nu]

    cps = sb_n // CHUNK
    for i, (h, sb) in enumerate(units):
        u = uws[i][:, :GDN_DV]
        w = uws[i][:, GDN_DV:].astype(BF16)
        a_intra = (kq[i][sb_n:] * decays[i]).astype(BF16)
        q_dec = (qs[i] * egcs[i]).astype(BF16)
        kd_t = (ks[i] * jnp.exp(glbs[i] - gcbs[i])).T.astype(BF16)
        eg = jnp.exp(glbs[i])
        for cc in range(cps):
            c = sb * cps + cc
            rs = slice(cc * CHUNK, (cc + 1) * CHUNK)
            wq_ref[0, h, c, 0:CHUNK, :] = w[rs]
            wq_ref[0, h, c, CHUNK:2 * CHUNK, :] = q_dec[rs]
            u_ref[0, h, c] = u[rs]
            aq_ref[0, h, c] = a_intra[rs, rs]
            kdt_ref[0, h, c] = kd_t[:, rs]
            eg_ref[0, h, c] = eg[cc * CHUNK: cc * CHUNK + 1, :]


def _gdn_prep(proj, conv_w, a_log_pat, dtb_pat, a_log_col, dtb_col, b, t):
    nblk = t // GDN_BLK
    nc = t // CHUNK
    c3 = 2 * GDN_QK + GDN_V
    small = lambda s: pl.BlockSpec(s, lambda i: (0,) * len(s))
    omap = lambda i: (i // nblk, 0, i % nblk, 0, 0)
    return pl.pallas_call(
        functools.partial(_gdnprep_kernel, nblk=nblk),
        grid=(b * nblk,),
        in_specs=[
            pl.BlockSpec((GDN_BLK, c3), lambda i: (i, COL_QKV // c3)),
            pl.BlockSpec((8, c3), lambda i: (jnp.maximum(i * (GDN_BLK // 8) - 1, 0), COL_QKV // c3)),
            pl.BlockSpec((GDN_BLK, LANE), lambda i: (i, COL_AB // LANE)),
            small((CONV_K, c3)), small((1, LANE)), small((1, LANE)), small((8, 1)), small((8, 1)),
        ],
        out_specs=[
            pl.BlockSpec((1, GDN_HEADS, CPB, 2 * CHUNK, GDN_DK), omap),
            pl.BlockSpec((1, GDN_HEADS, CPB, CHUNK, GDN_DV), omap),
            pl.BlockSpec((1, GDN_HEADS, CPB, CHUNK, CHUNK), omap),
            pl.BlockSpec((1, GDN_HEADS, CPB, GDN_DK, CHUNK), omap),
            pl.BlockSpec((1, GDN_HEADS, CPB, 1, GDN_DV), omap),
        ],
        out_shape=[
            jax.ShapeDtypeStruct((b, GDN_HEADS, nc, 2 * CHUNK, GDN_DK), BF16),
            jax.ShapeDtypeStruct((b, GDN_HEADS, nc, CHUNK, GDN_DV), F32),
            jax.ShapeDtypeStruct((b, GDN_HEADS, nc, CHUNK, CHUNK), BF16),
            jax.ShapeDtypeStruct((b, GDN_HEADS, nc, GDN_DK, CHUNK), BF16),
            jax.ShapeDtypeStruct((b, GDN_HEADS, nc, 1, GDN_DV), F32),
        ],
        compiler_params=_cparams(("parallel",)),
        name="gdn_prep",
    )(proj, proj, proj, conv_w, a_log_pat, dtb_pat, a_log_col, dtb_col)


def _gdnscan_kernel(wq_ref, u_ref, aq_ref, kdt_ref, eg_ref, o_ref, s_scr, *, nb, cb):
    @pl.when(pl.program_id(0) == 0)
    def _():
        s_scr[...] = jnp.zeros(s_scr.shape, F32)

    streams = [(bi, h) for bi in range(nb) for h in range(GDN_HEADS)]

    def chunk(c, carry):
        rs = [_dot(wq_ref[bi, h, c], s_scr[bi, h].astype(BF16)) for bi, h in streams]
        vbs = [(u_ref[bi, h, c] - r[:CHUNK]).astype(BF16) for (bi, h), r in zip(streams, rs)]
        rows = pl.ds(pl.multiple_of(c * CHUNK, CHUNK), CHUNK)
        for (bi, h), r, vb in zip(streams, rs, vbs):
            o_ref[bi, h, rows, :] = r[CHUNK:] + _dot(aq_ref[bi, h, c], vb)
        for (bi, h), vb in zip(streams, vbs):
            s_scr[bi, h] = s_scr[bi, h] * eg_ref[bi, h, c] + _dot(kdt_ref[bi, h, c], vb)
        return carry

    lax.fori_loop(0, cb, chunk, 0)


def _gdn_scan(wq, u, aq, kdt, eg, cb):
    b, _, nc = wq.shape[:3]
    spec = lambda d0, d1: pl.BlockSpec((b, GDN_HEADS, cb, d0, d1), lambda i: (0, 0, i, 0, 0))
    return pl.pallas_call(
        functools.partial(_gdnscan_kernel, nb=b, cb=cb),
        grid=(nc // cb,),
        in_specs=[spec(2 * CHUNK, GDN_DK), spec(CHUNK, GDN_DV), spec(CHUNK, CHUNK),
                  spec(GDN_DK, CHUNK), spec(1, GDN_DV)],
        out_specs=pl.BlockSpec((b, GDN_HEADS, cb * CHUNK, GDN_DV), lambda i: (0, 0, i, 0)),
        out_shape=jax.ShapeDtypeStruct((b, GDN_HEADS, nc * CHUNK, GDN_DV), F32),
        scratch_shapes=[pltpu.VMEM((b, GDN_HEADS, GDN_DK, GDN_DV), F32)],
        compiler_params=_cparams(("arbitrary",)),
        name="gdn_scan",
    )(wq, u, aq, kdt, eg)


def _merge_kernel(x_ref, an_ref, wgz_ref, oa_ref, ob_ref, gn_ref, woa_ref, wob_ref, wout_ref, o_ref):
    x = x_ref[...]
    ms = jnp.mean(x * x, axis=-1, keepdims=True)
    hn = (x * lax.rsqrt(ms + EPS) * an_ref[...]).astype(BF16)
    gz = _dot(hn, wgz_ref[...])
    parts = []
    for h in range(GDN_HEADS):
        o = oa_ref[0, h]
        ms = jnp.mean(o * o, axis=-1, keepdims=True)
        z = gz[:, h * GDN_DV:(h + 1) * GDN_DV]
        parts.append((o * lax.rsqrt(ms + EPS) * gn_ref[...] * _silu(z)).astype(BF16))
    y_a = _dot(jnp.concatenate(parts, axis=1), woa_ref[...])
    y_b = _dot(ob_ref[...], wob_ref[...])
    merged = (_sigmoid(gz[:, GDN_V:GDN_V + D_MODEL]) * y_a + _sigmoid(gz[:, GDN_V + D_MODEL:]) * y_b)
    o_ref[...] = x + _dot(merged.astype(BF16), wout_ref[...])


def _merge(x2d, attn_gain, w_gz, o_a, o_b, gdn_gain, w_o_a, w_o_b, w_out, t, tm):
    m = x2d.shape[0]
    npb = t // tm
    small = lambda s: pl.BlockSpec(s, lambda i: (0,) * len(s), pipeline_mode=pl.Buffered(1))
    return pl.pallas_call(
        _merge_kernel,
        grid=(m // tm,),
        in_specs=[
            pl.BlockSpec((tm, D_MODEL), lambda i: (i, 0)),
            small((1, D_MODEL)), small((D_MODEL, N_GZ)),
            pl.BlockSpec((1, GDN_HEADS, tm, GDN_DV), lambda i: (i // npb, 0, i % npb, 0)),
            pl.BlockSpec((tm, DIFF_V), lambda i: (i, 0)),
            small((1, GDN_DV)), small((GDN_V, D_MODEL)), small((DIFF_V, D_MODEL)), small((D_MODEL, D_MODEL)),
        ],
        out_specs=pl.BlockSpec((tm, D_MODEL), lambda i: (i, 0)),
        out_shape=jax.ShapeDtypeStruct((m, D_MODEL), F32),
        compiler_params=_cparams(("parallel",)),
        name="merge",
    )(x2d, attn_gain, w_gz, o_a, o_b, gdn_gain, w_o_a, w_o_b, w_out)


def _mlp_kernel(x_ref, mg_ref, wup_ref, wdn_ref, pg_ref, wpg_ref, p_ref, wp_ref, o_ref, h_scr, acc_scr):
    f = pl.program_id(1)

    @pl.when(f == 0)
    def _():
        x = x_ref[...]
        ms = jnp.mean(x * x, axis=-1, keepdims=True)
        h_scr[...] = (x * lax.rsqrt(ms + EPS) * mg_ref[...]).astype(BF16)
        acc_scr[...] = jnp.zeros(acc_scr.shape, F32)

    a = jnp.maximum(_dot(h_scr[...], wup_ref[...]), 0.0)
    acc_scr[...] += _dot((a * a).astype(BF16), wdn_ref[...])

    @pl.when(f == pl.num_programs(1) - 1)
    def _():
        x2 = x_ref[...] + acc_scr[...]
        ms = jnp.mean(x2 * x2, axis=-1, keepdims=True)
        hn = (x2 * lax.rsqrt(ms + EPS) * pg_ref[...]).astype(BF16)
        gate = _sigmoid(_dot(hn, wpg_ref[...]))
        o_ref[...] = x2 + gate * _dot(p_ref[0, 0].astype(BF16), wp_ref[...])


def _mlp_ple(x1, mlp_gain, w_up, w_down, ple_gain, w_ple_gate, p, layer, w_ple, tm, tf):
    m = x1.shape[0]
    npb = p.shape[2] // tm
    small = lambda s: pl.BlockSpec(s, lambda i, f: (0,) * len(s))
    return pl.pallas_call(
        _mlp_kernel,
        grid=(m // tm, D_FF // tf),
        in_specs=[
            pl.BlockSpec((tm, D_MODEL), lambda i, f: (i, 0)),
            small((1, D_MODEL)),
            pl.BlockSpec((D_MODEL, tf), lambda i, f: (0, f)),
            pl.BlockSpec((tf, D_MODEL), lambda i, f: (f, 0)),
            small((1, D_MODEL)), small((D_MODEL, D_MODEL)),
            pl.BlockSpec((1, 1, tm, PLE_DIM), lambda i, f: (layer, i // npb, i % npb, 0)),
            small((PLE_DIM, D_MODEL)),
        ],
        out_specs=pl.BlockSpec((tm, D_MODEL), lambda i, f: (i, 0)),
        out_shape=jax.ShapeDtypeStruct((m, D_MODEL), F32),
        scratch_shapes=[pltpu.VMEM((tm, D_MODEL), BF16), pltpu.VMEM((tm, D_MODEL), F32)],
        compiler_params=_cparams(("parallel", "arbitrary")),
        name="mlp_ple",
    )(x1, mlp_gain, w_up, w_down, ple_gain, w_ple_gate, p, w_ple)


def _split_w_in(w):
    o_z = 3 * GDN_QK
    o_a = o_z + GDN_V
    o_dq = o_a + 2 * GDN_HEADS
    o_gate = o_dq + 3 * DIFF_QK
    pad = jnp.zeros((D_MODEL, N_PAD - COL_AB - 2 * GDN_HEADS), w.dtype)
    w_main = jnp.concatenate([w[:, :o_z], w[:, o_dq:o_gate], w[:, o_a:o_dq], pad], axis=1).astype(BF16)
    w_gz = jnp.concatenate([w[:, o_z:o_a], w[:, o_gate:]], axis=1).astype(BF16)
    return w_main, w_gz


def _lane_pat(vec, offset):
    return jnp.zeros((1, LANE), F32).at[0, offset:offset + GDN_HEADS].set(vec.astype(F32))


def _col_pat(vec):
    return jnp.zeros((8, 1), F32).at[:GDN_HEADS, 0].set(vec.astype(F32))


def _pick(n, pref):
    return pref if n % pref == 0 else n


def kernel(x, p, positions, attn_norm, w_in, conv_w, a_log, dt_bias, gdn_norm, w_o_a, q_norm, k_norm,
           lambda_q1, lambda_k1, lambda_q2, lambda_k2, diff_norm, w_o_b, w_out, mlp_norm, w_up, w_down,
           ple_norm, w_ple_gate, w_ple):
    b, t, _ = x.shape
    m = b * t
    depth = w_in.shape[0]
    assert t % GDN_BLK == 0, "sequence length must be a multiple of the gdn_prep block"

    tm_big = _pick(m, 1024)
    tm_mid = _pick(m, 512)
    tq = _pick(t, 512)
    tk = _pick(t, 512)
    assert tk % tq == 0
    cb = _pick(t // CHUNK, 8)
    pos3 = positions.reshape(m // tk, 1, tk)
    freq = (ROPE_THETA ** (-jnp.arange(0, ROT_DIM, 2, dtype=F32) / ROT_DIM)).reshape(ROT_DIM // 2, 1)
    row = lambda v: v.astype(F32).reshape(1, -1)

    x2d = x.reshape(m, D_MODEL)
    for i in range(depth):
        lam_init = 0.8 - 0.6 * math.exp(-0.3 * i)
        w_main, w_gz = _split_w_in(w_in[i])
        proj = _in_proj(x2d, row(attn_norm[i]), w_main, tm_mid, 512)

        dqt, dk, dvt = _diff_prep(proj, pos3, freq, jnp.tile(row(q_norm[i]), (1, DIFF_QK // DIFF_DH)),
                                  jnp.tile(row(k_norm[i]), (1, DIFF_QK // DIFF_DH)), tk)
        o_b = _flash_diff(dqt, dk, dvt, row(lambda_q1[i]), row(lambda_k1[i]), row(lambda_q2[i]),
                          row(lambda_k2[i]), row(diff_norm[i]), b, t, tq, tk, lam_init)

        wq, u, aq, kdt, eg = _gdn_prep(proj, conv_w[i].astype(F32), _lane_pat(a_log[i], 0),
                                       _lane_pat(dt_bias[i], 0), _col_pat(a_log[i]), _col_pat(dt_bias[i]), b, t)
        o_a = _gdn_scan(wq, u, aq, kdt, eg, cb)

        x1 = _merge(x2d, row(attn_norm[i]), w_gz, o_a, o_b, row(gdn_norm[i]), w_o_a[i].astype(BF16),
                    w_o_b[i].astype(BF16), w_out[i].astype(BF16), t, _pick(t, 512))
        x2d = _mlp_ple(x1, row(mlp_norm[i]), w_up[i].astype(BF16), w_down[i].astype(BF16), row(ple_norm[i]),
                       w_ple_gate[i].astype(BF16), p, i, w_ple[i].astype(BF16), _pick(t, 1024), 1024)
    return x2d.reshape(b, t, D_MODEL)
```

```python
import functools
import math

import jax
import jax.numpy as jnp
from jax import lax
from jax.experimental import pallas as pl
from jax.experimental.pallas import tpu as pltpu

F32 = jnp.float32
BF16 = jnp.bfloat16

D_MODEL = 1024
PLE_DIM = 256
EPS = 1e-6
GDN_HEADS = 4
GDN_DK = 128
GDN_DV = 128
CONV_K = 4
CHUNK = 64
DIFF_HEADS = 4
DIFF_DH = 64
DIFF_DV = 2 * DIFF_DH
ROT_DIM = DIFF_DH // 4
ROPE_THETA = 500000.0
D_FF = 4 * D_MODEL

GDN_QK = GDN_HEADS * GDN_DK
GDN_V = GDN_HEADS * GDN_DV
DIFF_QK = DIFF_HEADS * 2 * DIFF_DH
DIFF_V = DIFF_HEADS * DIFF_DV
D_IN = 4 * GDN_QK + 2 * GDN_HEADS + 3 * DIFF_QK + 2 * D_MODEL

COL_QKV = 0
COL_DQ = 1536
COL_DK = 2048
COL_DV = 2560
COL_AB = 3072
N_PAD = COL_AB + 128
N_GZ = GDN_V + 2 * D_MODEL

LANE = 128
GDN_BLK = 512
CPB = GDN_BLK // CHUNK
FLASH_HPB = 4
LOG2E = math.log2(math.e)
ACC_ROWS = DIFF_DV + 16
QK_LAG = 0
PV_LAG = 1
INV_BASE = 16
INV_PASSES = 1
VMEM_LIMIT = 48 * 1024 * 1024
FLASH_VMEM_LIMIT = 58 * 1024 * 1024

HI = lax.Precision.HIGHEST


def _cparams(sem):
    return pltpu.CompilerParams(dimension_semantics=sem, vmem_limit_bytes=VMEM_LIMIT)


def _dot(a, b):
    return jnp.dot(a, b, preferred_element_type=F32)


def _dot_nt(a, b):
    return lax.dot_general(a, b, (((1,), (1,)), ((), ())), preferred_element_type=F32)


def _split_bf16(a):
    hi = a.astype(BF16)
    lo = (a - hi.astype(F32)).astype(BF16)
    return hi, lo


def _dot3(a, b):
    ah, al = _split_bf16(a)
    bh, bl = _split_bf16(b)
    return _dot(ah, bh) + _dot(ah, bl) + _dot(al, bh)


def _tdot(a, b):
    if INV_PASSES == 3:
        return _dot3(a, b)
    return _dot(a.astype(BF16), b.astype(BF16))


def _sigmoid(x):
    return 1.0 / (1.0 + jnp.exp(-x))


def _silu(x):
    return x * _sigmoid(x)


def _softplus(x):
    return jnp.maximum(x, 0.0) + jnp.log(1.0 + jnp.exp(-jnp.abs(x)))


def _inproj_kernel(x_ref, g_ref, w_ref, o_ref, *, tn):
    x = x_ref[...]
    ms = jnp.mean(x * x, axis=-1, keepdims=True)
    h = (x * lax.rsqrt(ms + EPS) * g_ref[...]).astype(BF16)
    for lo in range(0, N_PAD, tn):
        hi = min(lo + tn, N_PAD)
        o_ref[:, lo:hi] = _dot(h, w_ref[:, lo:hi])


def _in_proj(x2d, gain, w_bf16, tm, tn):
    m = x2d.shape[0]
    return pl.pallas_call(
        functools.partial(_inproj_kernel, tn=tn),
        grid=(m // tm,),
        in_specs=[
            pl.BlockSpec((tm, D_MODEL), lambda i: (i, 0)),
            pl.BlockSpec((1, D_MODEL), lambda i: (0, 0)),
            pl.BlockSpec((D_MODEL, N_PAD), lambda i: (0, 0), pipeline_mode=pl.Buffered(1)),
        ],
        out_specs=pl.BlockSpec((tm, N_PAD), lambda i: (i, 0)),
        out_shape=jax.ShapeDtypeStruct((m, N_PAD), F32),
        compiler_params=_cparams(("parallel",)),
        name="in_proj",
    )(x2d, gain, w_bf16)


def _diffprep_kernel(pos_ref, freq_ref, q_ref, k_ref, v_ref, qg_ref, kg_ref, oq_ref, ok_ref, ov_ref):
    tm = q_ref.shape[0]
    ang = freq_ref[...] * pos_ref[0].astype(F32)
    cos_t, sin_t = jnp.cos(ang), jnp.sin(ang)
    half = ROT_DIM // 2
    fi = jnp.bitwise_and(lax.broadcasted_iota(jnp.int32, (3 * half, LANE), 0), half - 1)
    d = jnp.bitwise_and(lax.broadcasted_iota(jnp.int32, (3 * half, LANE), 1), DIFF_DH - 1)
    e_lo = jnp.where(d == fi, 1.0, 0.0).astype(BF16)
    e_hi = jnp.where(d == fi + half, 1.0, 0.0).astype(BF16)

    def expand(table_t, sel):
        p0 = table_t.astype(BF16).astype(F32)
        r1 = table_t - p0
        p1 = r1.astype(BF16).astype(F32)
        p2 = (r1 - p1).astype(BF16).astype(F32)
        pieces = jnp.concatenate([p0, p1, p2], axis=0).astype(BF16)
        out = lax.dot_general(pieces, sel, (((0,), (0,)), ((), ())), preferred_element_type=F32)
        return jnp.concatenate([out] * (DIFF_QK // LANE), axis=1)

    lane = lax.broadcasted_iota(jnp.int32, (1, DIFF_QK), 1)
    passthrough = jnp.where(jnp.bitwise_and(lane, DIFF_DH - 1) >= ROT_DIM, 1.0, 0.0).astype(F32)
    cos_f = expand(cos_t, e_lo + e_hi) + passthrough
    sin_a = -expand(sin_t, e_lo)
    sin_b = expand(sin_t, e_hi)

    ri = lax.broadcasted_iota(jnp.int32, (DIFF_QK, DIFF_QK), 0)
    ci = lax.broadcasted_iota(jnp.int32, (DIFF_QK, DIFF_QK), 1)
    grp = jnp.where(jnp.right_shift(ri, 6) == jnp.right_shift(ci, 6), 1.0 / DIFF_DH, 0.0).astype(BF16)

    def norm_rot(x, gain):
        sq = x * x
        hi, lo = _split_bf16(sq)
        ms = _dot(hi, grp) + _dot(lo, grp)
        y = x * lax.rsqrt(ms + EPS) * gain
        up = pltpu.roll(y, DIFF_QK - half, 1)
        dn = pltpu.roll(y, half, 1)
        return y * cos_f + up * sin_a + dn * sin_b

    oq_ref[0] = norm_rot(q_ref[...], qg_ref[...] * (DIFF_DH ** -0.5 * LOG2E)).T.astype(BF16)
    ok_ref[...] = norm_rot(k_ref[...], kg_ref[...]).astype(BF16)
    ov_ref[0] = v_ref[...].T.astype(BF16)


def _diff_prep(proj, pos3, freq, qg, kg, tm):
    m = proj.shape[0]
    blk = lambda c: pl.BlockSpec((tm, DIFF_QK), lambda i: (i, c))
    small = lambda s: pl.BlockSpec(s, lambda i: (0,) * len(s))
    out_t = jax.ShapeDtypeStruct((m // tm, DIFF_QK, tm), BF16)
    spec_t = pl.BlockSpec((1, DIFF_QK, tm), lambda i: (i, 0, 0))
    return pl.pallas_call(
        _diffprep_kernel,
        grid=(m // tm,),
        in_specs=[
            pl.BlockSpec((1, 1, tm), lambda i: (i, 0, 0)),
            small((ROT_DIM // 2, 1)),
            blk(COL_DQ // DIFF_QK), blk(COL_DK // DIFF_QK), blk(COL_DV // DIFF_QK),
            small((1, DIFF_QK)), small((1, DIFF_QK)),
        ],
        out_specs=[spec_t, pl.BlockSpec((tm, DIFF_QK), lambda i: (i, 0)), spec_t],
        out_shape=[out_t, jax.ShapeDtypeStruct((m, DIFF_QK), BF16), out_t],
        compiler_params=_cparams(("parallel",)),
        name="diff_prep",
    )(pos3, freq, proj, proj, proj, qg, kg)


def _flash_kernel(qt_ref, k_ref, vt_ref, lq1_ref, lk1_ref, lq2_ref, lk2_ref, gain_ref, o_ref,
                  *scratch, tq, tk, lam_init):
    qi = pl.program_id(2)
    chains = [(hh, c) for hh in range(FLASH_HPB) for c in range(2)]
    acc_scr, s_scr, p_scr = (scratch[k * len(chains):(k + 1) * len(chains)] for k in range(3))
    qst = {}
    for hh in range(FLASH_HPB):
        qt = qt_ref[0, hh * LANE:(hh + 1) * LANE, :]
        row = lax.broadcasted_iota(jnp.int32, qt.shape, 0)
        zero = jnp.zeros_like(qt)
        qst[hh, 0] = jnp.where(row < DIFF_DH, qt, zero)
        qst[hh, 1] = jnp.where(row >= DIFF_DH, qt, zero)

    nch = len(chains)
    last = nch - 1
    for c in range(nch):
        acc_scr[c][...] = jnp.zeros((ACC_ROWS, tq), F32)
    ones_rows = jnp.ones((ACC_ROWS - DIFF_DV, tk), BF16)

    def qk(c, j):
        hh, comp = chains[c]
        start = pl.multiple_of(j * tk, tk)
        s = _dot(k_ref[pl.ds(start, tk), hh * LANE:(hh + 1) * LANE], qst[hh, comp])
        s_scr[c][...] = s
        return jnp.max(s, axis=0, keepdims=True)

    def pv(c, j, alpha):
        hh, comp = chains[c]
        vt_aug = jnp.concatenate([vt_ref[j, hh * LANE:(hh + 1) * LANE, :], ones_rows], axis=0)
        upd = _dot(vt_aug, p_scr[c][...])
        acc_scr[c][...] = alpha * acc_scr[c][...] + upd

    def softmax(c, j, m_old, cmax, masked):
        s = s_scr[c][...]
        if masked:
            r = lax.broadcasted_iota(jnp.int32, (tk, tq), 0)
            col = lax.broadcasted_iota(jnp.int32, (tk, tq), 1)
            s = jnp.where(j * tk + r <= qi * tq + col, s, -jnp.inf)
            cmax = jnp.max(s, axis=0, keepdims=True)
        m_new = jnp.maximum(m_old, cmax)
        alpha = jnp.exp2(m_old - m_new)
        p_scr[c][...] = jnp.exp2(s - m_new).astype(BF16)
        return m_new, alpha

    n_full = (qi * tq) // tk
    n_early = nch - max(QK_LAG, 1)
    cmax0 = tuple(qk(c, 0) for c in range(n_early))
    for c in range(nch - PV_LAG, nch):
        p_scr[c][...] = jnp.zeros((tk, tq), BF16)

    def trip(i, carry):
        stats, cmaxs, alpha_tail = carry
        stats, cmaxs = list(stats), list(cmaxs)
        cmaxs += [qk(c, i) for c in range(n_early, nch)]
        for k, c in enumerate(range(nch - PV_LAG, nch)):
            pv(c, jnp.maximum(i - 1, 0), alpha_tail[k])
        alphas = [None] * nch
        for c in range(nch):
            stats[c], alphas[c] = softmax(c, i, stats[c], cmaxs[c], False)
            if 0 <= c - QK_LAG < n_early:
                cmaxs[c - QK_LAG] = qk(c - QK_LAG, i + 1)
            if c >= PV_LAG:
                pv(c - PV_LAG, i, alphas[c - PV_LAG])
        return tuple(stats), tuple(cmaxs[:n_early]), tuple(alphas[nch - PV_LAG:])

    init = (tuple(jnp.full((1, tq), -jnp.inf, F32) for _ in chains), cmax0,
            tuple(jnp.ones((1, tq), F32) for _ in range(PV_LAG)))
    stats, _, alpha_tail = lax.fori_loop(0, n_full, trip, init)
    stats = list(stats)
    for c in range(n_early, nch):
        qk(c, n_full)
    for k, c in enumerate(range(nch - PV_LAG, nch)):
        pv(c, jnp.maximum(n_full - 1, 0), alpha_tail[k])
    for c in range(nch):
        stats[c], alpha = softmax(c, n_full, stats[c], None, True)
        pv(c, n_full, alpha)

    lam = (jnp.exp(jnp.sum(lq1_ref[...] * lk1_ref[...], axis=-1, keepdims=True))
           - jnp.exp(jnp.sum(lq2_ref[...] * lk2_ref[...], axis=-1, keepdims=True)) + lam_init)
    for hh in range(FLASH_HPB):
        a1 = acc_scr[2 * hh][...]
        a2 = acc_scr[2 * hh + 1][...]
        o1 = a1[:DIFF_DV] / a1[DIFF_DV:DIFF_DV + 1]
        o2 = a2[:DIFF_DV] / a2[DIFF_DV:DIFF_DV + 1]
        o = (o1 - lam * o2).T
        ms = jnp.mean(o * o, axis=-1, keepdims=True)
        o_ref[:, hh * LANE:(hh + 1) * LANE] = (
            o * lax.rsqrt(ms + EPS) * gain_ref[...] * (1.0 - lam_init)).astype(o_ref.dtype)


def _flash_diff(dqt, dk, dvt, lq1, lk1, lq2, lk2, gain, b, t, tq, tk, lam_init):
    nq = t // tq
    nkb = t // tk
    qpb = tk // tq
    hw = FLASH_HPB * LANE
    small = lambda s: pl.BlockSpec(s, lambda bi, h, qi: (0,) * len(s))
    return pl.pallas_call(
        functools.partial(_flash_kernel, tq=tq, tk=tk, lam_init=lam_init),
        grid=(b, DIFF_HEADS // FLASH_HPB, nq),
        in_specs=[
            pl.BlockSpec((1, hw, tq), lambda bi, h, qi: (bi * nkb + qi // qpb, h, qi % qpb)),
            pl.BlockSpec((t, hw), lambda bi, h, qi: (bi, h)),
            pl.BlockSpec((nkb, hw, tk), lambda bi, h, qi: (bi, h, 0)),
            small((1, DIFF_DH)), small((1, DIFF_DH)), small((1, DIFF_DH)), small((1, DIFF_DH)),
            small((1, DIFF_DV)),
        ],
        out_specs=pl.BlockSpec((tq, hw), lambda bi, h, qi: (bi * nq + qi, h)),
        out_shape=jax.ShapeDtypeStruct((b * t, DIFF_V), BF16),
        scratch_shapes=([pltpu.VMEM((ACC_ROWS, tq), F32)] * (2 * FLASH_HPB)
                        + [pltpu.VMEM((tk, tq), F32)] * (2 * FLASH_HPB)
                        + [pltpu.VMEM((tk, tq), BF16)] * (2 * FLASH_HPB)),
        compiler_params=pltpu.CompilerParams(dimension_semantics=("parallel", "parallel", "arbitrary"),
                                             vmem_limit_bytes=FLASH_VMEM_LIMIT),
        name="flash_diff",
    )(dqt, dk, dvt, lq1, lk1, lq2, lk2, gain)


def _gdnprep_kernel(cur_ref, halo_ref, ab_ref, convw_ref, alog_ref, dtb_ref, alogc_ref, dtbc_ref,
                    wq_ref, u_ref, aq_ref, kdt_ref, eg_ref, *, nblk):
    i = pl.program_id(0)
    n = GDN_BLK
    keep = jnp.where(i % nblk == 0, 0.0, 1.0).astype(F32)
    xs = jnp.concatenate([halo_ref[...] * keep, cur_ref[...]], axis=0)
    cw = convw_ref[...]
    conv = xs[8:8 + n] * cw[CONV_K - 1:CONV_K]
    for j in range(CONV_K - 1):
        off = 8 - (CONV_K - 1) + j
        conv = conv + xs[off:off + n] * cw[j:j + 1]
    qkv = _silu(conv)

    ab = ab_ref[...]
    ab_t = ab.T
    g_col = -jnp.exp(alog_ref[...]) * _softplus(ab + dtb_ref[...])
    beta_col = _sigmoid(ab)
    g_row = -jnp.exp(alogc_ref[...]) * _softplus(ab_t[:8] + dtbc_ref[...])

    sb_n = LANE
    ri = lax.broadcasted_iota(jnp.int32, (sb_n, sb_n), 0)
    ci = lax.broadcasted_iota(jnp.int32, (sb_n, sb_n), 1)
    same = jnp.right_shift(ri, 6) == jnp.right_shift(ci, 6)
    same32 = jnp.right_shift(ri, 5) == jnp.right_shift(ci, 5)
    same16 = jnp.right_shift(ri, 4) == jnp.right_shift(ci, 4)
    incl = jnp.logical_and(same, ri >= ci)
    strict = jnp.logical_and(same, ri > ci)
    one = jnp.ones((sb_n, sb_n), F32)
    zero = jnp.zeros((sb_n, sb_n), F32)
    tri = jnp.where(incl, one, zero).astype(BF16)
    blk_ones = jnp.where(same, one, zero).astype(BF16)
    eye = jnp.where(ri == ci, one, zero)

    def split3(a):
        p0 = a.astype(BF16)
        r1 = a - p0.astype(F32)
        p1 = r1.astype(BF16)
        return p0, p1, (r1 - p1.astype(F32)).astype(BF16)

    units = [(h, sb) for sb in range(n // sb_n) for h in range(GDN_HEADS)]
    gc_cols, gl_cols, gc_rows = [], [], []
    for sb in range(n // sb_n):
        rows = slice(sb * sb_n, (sb + 1) * sb_n)
        pc = split3(g_col[rows])
        pr = split3(g_row[:, rows])
        gc_cols.append(sum(_dot(tri, p) for p in pc))
        gl_cols.append(sum(_dot(blk_ones, p) for p in pc))
        gc_rows.append(sum(_dot_nt(p, tri) for p in pr))

    qs, ks, kbs, kbfs, decays, egcs, glbs, gcbs, rhss = [], [], [], [], [], [], [], [], []
    for h, sb in units:
        rows = slice(sb * sb_n, (sb + 1) * sb_n)
        q = qkv[rows, h * GDN_DK:(h + 1) * GDN_DK]
        k = qkv[rows, GDN_QK + h * GDN_DK: GDN_QK + (h + 1) * GDN_DK]
        v = qkv[rows, 2 * GDN_QK + h * GDN_DV: 2 * GDN_QK + (h + 1) * GDN_DV]
        q = q * lax.rsqrt(jnp.sum(q * q, axis=-1, keepdims=True) + EPS) * (GDN_DK ** -0.5)
        k = k * lax.rsqrt(jnp.sum(k * k, axis=-1, keepdims=True) + EPS)
        beta = beta_col[rows, GDN_HEADS + h: GDN_HEADS + h + 1]
        gcb = jnp.broadcast_to(gc_cols[sb][:, h:h + 1], (sb_n, GDN_DK))
        glb = jnp.broadcast_to(gl_cols[sb][:, h:h + 1], (sb_n, GDN_DK))
        gdiff = gcb - gc_rows[sb][h:h + 1, :]
        kb = k * beta
        egc = jnp.exp(gcb)
        qs.append(q); ks.append(k); kbs.append(kb); kbfs.append(k.astype(BF16))
        decays.append(jnp.where(incl, jnp.exp(jnp.where(incl, gdiff, zero)), zero))
        egcs.append(egc); glbs.append(glb); gcbs.append(gcb)
        rhss.append(jnp.concatenate([v * beta, kb * egc], axis=1).astype(BF16))

    nu = range(len(units))
    kq = [_dot_nt(jnp.concatenate([kbs[i], qs[i]], axis=0).astype(BF16), kbfs[i]) for i in nu]
    lowers = [jnp.where(strict, kq[i][:sb_n] * decays[i], zero) for i in nu]
    xs_ = [jnp.where(same16, -lowers[i], zero) for i in nu]
    tmats = [eye + xs_[i] for i in nu]
    for _ in range(int(math.log2(INV_BASE)) - 1):
        xs_ = [_tdot(xs_[i], xs_[i]) for i in nu]
        tmats = [tmats[i] + _tdot(tmats[i], xs_[i]) for i in nu]
    for lo_mask, hi_mask in ((same16, same32), (same32, same)):
        sel = jnp.logical_and(hi_mask, jnp.logical_not(lo_mask))
        tmp = [_tdot(tmats[i], jnp.where(sel, lowers[i], zero)) for i in nu]
        tmats = [tmats[i] - _tdot(tmp[i], tmats[i]) for i in nu]
    uws = [_dot(tmats[i].astype(BF16), rhss[i]) for i in nu]

    cps = sb_n // CHUNK
    for i, (h, sb) in enumerate(units):
        u = uws[i][:, :GDN_DV]
        w = uws[i][:, GDN_DV:].astype(BF16)
        a_intra = (kq[i][sb_n:] * decays[i]).astype(BF16)
        q_dec = (qs[i] * egcs[i]).astype(BF16)
        kd_t = (ks[i] * jnp.exp(glbs[i] - gcbs[i])).T.astype(BF16)
        eg = jnp.exp(glbs[i])
        for cc in range(cps):
            c = sb * cps + cc
            rs = slice(cc * CHUNK, (cc + 1) * CHUNK)
            wq_ref[0, h, c, 0:CHUNK, :] = w[rs]
            wq_ref[0, h, c, CHUNK:2 * CHUNK, :] = q_dec[rs]
            u_ref[0, h, c] = u[rs]
            aq_ref[0, h, c] = a_intra[rs, rs]
            kdt_ref[0, h, c] = kd_t[:, rs]
            eg_ref[0, h, c] = eg[cc * CHUNK: cc * CHUNK + 1, :]


def _gdn_prep(proj, conv_w, a_log_pat, dtb_pat, a_log_col, dtb_col, b, t):
    nblk = t // GDN_BLK
    nc = t // CHUNK
    c3 = 2 * GDN_QK + GDN_V
    small = lambda s: pl.BlockSpec(s, lambda i: (0,) * len(s))
    omap = lambda i: (i // nblk, 0, i % nblk, 0, 0)
    return pl.pallas_call(
        functools.partial(_gdnprep_kernel, nblk=nblk),
        grid=(b * nblk,),
        in_specs=[
            pl.BlockSpec((GDN_BLK, c3), lambda i: (i, COL_QKV // c3)),
            pl.BlockSpec((8, c3), lambda i: (jnp.maximum(i * (GDN_BLK // 8) - 1, 0), COL_QKV // c3)),
            pl.BlockSpec((GDN_BLK, LANE), lambda i: (i, COL_AB // LANE)),
            small((CONV_K, c3)), small((1, LANE)), small((1, LANE)), small((8, 1)), small((8, 1)),
        ],
        out_specs=[
            pl.BlockSpec((1, GDN_HEADS, CPB, 2 * CHUNK, GDN_DK), omap),
            pl.BlockSpec((1, GDN_HEADS, CPB, CHUNK, GDN_DV), omap),
            pl.BlockSpec((1, GDN_HEADS, CPB, CHUNK, CHUNK), omap),
            pl.BlockSpec((1, GDN_HEADS, CPB, GDN_DK, CHUNK), omap),
            pl.BlockSpec((1, GDN_HEADS, CPB, 1, GDN_DV), omap),
        ],
        out_shape=[
            jax.ShapeDtypeStruct((b, GDN_HEADS, nc, 2 * CHUNK, GDN_DK), BF16),
            jax.ShapeDtypeStruct((b, GDN_HEADS, nc, CHUNK, GDN_DV), F32),
            jax.ShapeDtypeStruct((b, GDN_HEADS, nc, CHUNK, CHUNK), BF16),
            jax.ShapeDtypeStruct((b, GDN_HEADS, nc, GDN_DK, CHUNK), BF16),
            jax.ShapeDtypeStruct((b, GDN_HEADS, nc, 1, GDN_DV), F32),
        ],
        compiler_params=_cparams(("parallel",)),
        name="gdn_prep",
    )(proj, proj, proj, conv_w, a_log_pat, dtb_pat, a_log_col, dtb_col)


def _gdnscan_kernel(wq_ref, u_ref, aq_ref, kdt_ref, eg_ref, o_ref, s_scr, *, nb, cb):
    @pl.when(pl.program_id(0) == 0)
    def _():
        s_scr[...] = jnp.zeros(s_scr.shape, F32)

    streams = [(bi, h) for bi in range(nb) for h in range(GDN_HEADS)]

    def chunk(c, carry):
        rs = [_dot(wq_ref[bi, h, c], s_scr[bi, h].astype(BF16)) for bi, h in streams]
        vbs = [(u_ref[bi, h, c] - r[:CHUNK]).astype(BF16) for (bi, h), r in zip(streams, rs)]
        rows = pl.ds(pl.multiple_of(c * CHUNK, CHUNK), CHUNK)
        for (bi, h), r, vb in zip(streams, rs, vbs):
            o_ref[bi, h, rows, :] = r[CHUNK:] + _dot(aq_ref[bi, h, c], vb)
        for (bi, h), vb in zip(streams, vbs):
            s_scr[bi, h] = s_scr[bi, h] * eg_ref[bi, h, c] + _dot(kdt_ref[bi, h, c], vb)
        return carry

    lax.fori_loop(0, cb, chunk, 0)


def _gdn_scan(wq, u, aq, kdt, eg, cb):
    b, _, nc = wq.shape[:3]
    spec = lambda d0, d1: pl.BlockSpec((b, GDN_HEADS, cb, d0, d1), lambda i: (0, 0, i, 0, 0))
    return pl.pallas_call(
        functools.partial(_gdnscan_kernel, nb=b, cb=cb),
        grid=(nc // cb,),
        in_specs=[spec(2 * CHUNK, GDN_DK), spec(CHUNK, GDN_DV), spec(CHUNK, CHUNK),
                  spec(GDN_DK, CHUNK), spec(1, GDN_DV)],
        out_specs=pl.BlockSpec((b, GDN_HEADS, cb * CHUNK, GDN_DV), lambda i: (0, 0, i, 0)),
        out_shape=jax.ShapeDtypeStruct((b, GDN_HEADS, nc * CHUNK, GDN_DV), F32),
        scratch_shapes=[pltpu.VMEM((b, GDN_HEADS, GDN_DK, GDN_DV), F32)],
        compiler_params=_cparams(("arbitrary",)),
        name="gdn_scan",
    )(wq, u, aq, kdt, eg)


def _merge_kernel(x_ref, an_ref, wgz_ref, oa_ref, ob_ref, gn_ref, woa_ref, wob_ref, wout_ref, o_ref):
    x = x_ref[...]
    ms = jnp.mean(x * x, axis=-1, keepdims=True)
    hn = (x * lax.rsqrt(ms + EPS) * an_ref[...]).astype(BF16)
    gz = _dot(hn, wgz_ref[...])
    parts = []
    for h in range(GDN_HEADS):
        o = oa_ref[0, h]
        ms = jnp.mean(o * o, axis=-1, keepdims=True)
        z = gz[:, h * GDN_DV:(h + 1) * GDN_DV]
        parts.append((o * lax.rsqrt(ms + EPS) * gn_ref[...] * _silu(z)).astype(BF16))
    y_a = _dot(jnp.concatenate(parts, axis=1), woa_ref[...])
    y_b = _dot(ob_ref[...], wob_ref[...])
    merged = (_sigmoid(gz[:, GDN_V:GDN_V + D_MODEL]) * y_a + _sigmoid(gz[:, GDN_V + D_MODEL:]) * y_b)
    o_ref[...] = x + _dot(merged.astype(BF16), wout_ref[...])


def _merge(x2d, attn_gain, w_gz, o_a, o_b, gdn_gain, w_o_a, w_o_b, w_out, t, tm):
    m = x2d.shape[0]
    npb = t // tm
    small = lambda s: pl.BlockSpec(s, lambda i: (0,) * len(s), pipeline_mode=pl.Buffered(1))
    return pl.pallas_call(
        _merge_kernel,
        grid=(m // tm,),
        in_specs=[
            pl.BlockSpec((tm, D_MODEL), lambda i: (i, 0)),
            small((1, D_MODEL)), small((D_MODEL, N_GZ)),
            pl.BlockSpec((1, GDN_HEADS, tm, GDN_DV), lambda i: (i // npb, 0, i % npb, 0)),
            pl.BlockSpec((tm, DIFF_V), lambda i: (i, 0)),
            small((1, GDN_DV)), small((GDN_V, D_MODEL)), small((DIFF_V, D_MODEL)), small((D_MODEL, D_MODEL)),
        ],
        out_specs=pl.BlockSpec((tm, D_MODEL), lambda i: (i, 0)),
        out_shape=jax.ShapeDtypeStruct((m, D_MODEL), F32),
        compiler_params=_cparams(("parallel",)),
        name="merge",
    )(x2d, attn_gain, w_gz, o_a, o_b, gdn_gain, w_o_a, w_o_b, w_out)


def _mlp_kernel(x_ref, mg_ref, wup_ref, wdn_ref, pg_ref, wpg_ref, p_ref, wp_ref, o_ref, h_scr, acc_scr):
    f = pl.program_id(1)

    @pl.when(f == 0)
    def _():
        x = x_ref[...]
        ms = jnp.mean(x * x, axis=-1, keepdims=True)
        h_scr[...] = (x * lax.rsqrt(ms + EPS) * mg_ref[...]).astype(BF16)
        acc_scr[...] = jnp.zeros(acc_scr.shape, F32)

    a = jnp.maximum(_dot(h_scr[...], wup_ref[...]), 0.0)
    acc_scr[...] += _dot((a * a).astype(BF16), wdn_ref[...])

    @pl.when(f == pl.num_programs(1) - 1)
    def _():
        x2 = x_ref[...] + acc_scr[...]
        ms = jnp.mean(x2 * x2, axis=-1, keepdims=True)
        hn = (x2 * lax.rsqrt(ms + EPS) * pg_ref[...]).astype(BF16)
        gate = _sigmoid(_dot(hn, wpg_ref[...]))
        o_ref[...] = x2 + gate * _dot(p_ref[0, 0].astype(BF16), wp_ref[...])


def _mlp_ple(x1, mlp_gain, w_up, w_down, ple_gain, w_ple_gate, p, layer, w_ple, tm, tf):
    m = x1.shape[0]
    npb = p.shape[2] // tm
    small = lambda s: pl.BlockSpec(s, lambda i, f: (0,) * len(s))
    return pl.pallas_call(
        _mlp_kernel,
        grid=(m // tm, D_FF // tf),
        in_specs=[
            pl.BlockSpec((tm, D_MODEL), lambda i, f: (i, 0)),
            small((1, D_MODEL)),
            pl.BlockSpec((D_MODEL, tf), lambda i, f: (0, f)),
            pl.BlockSpec((tf, D_MODEL), lambda i, f: (f, 0)),
            small((1, D_MODEL)), small((D_MODEL, D_MODEL)),
            pl.BlockSpec((1, 1, tm, PLE_DIM), lambda i, f: (layer, i // npb, i % npb, 0)),
            small((PLE_DIM, D_MODEL)),
        ],
        out_specs=pl.BlockSpec((tm, D_MODEL), lambda i, f: (i, 0)),
        out_shape=jax.ShapeDtypeStruct((m, D_MODEL), F32),
        scratch_shapes=[pltpu.VMEM((tm, D_MODEL), BF16), pltpu.VMEM((tm, D_MODEL), F32)],
        compiler_params=_cparams(("parallel", "arbitrary")),
        name="mlp_ple",
    )(x1, mlp_gain, w_up, w_down, ple_gain, w_ple_gate, p, w_ple)


def _split_w_in(w):
    o_z = 3 * GDN_QK
    o_a = o_z + GDN_V
    o_dq = o_a + 2 * GDN_HEADS
    o_gate = o_dq + 3 * DIFF_QK
    pad = jnp.zeros((D_MODEL, N_PAD - COL_AB - 2 * GDN_HEADS), w.dtype)
    w_main = jnp.concatenate([w[:, :o_z], w[:, o_dq:o_gate], w[:, o_a:o_dq], pad], axis=1).astype(BF16)
    w_gz = jnp.concatenate([w[:, o_z:o_a], w[:, o_gate:]], axis=1).astype(BF16)
    return w_main, w_gz


def _lane_pat(vec, offset):
    return jnp.zeros((1, LANE), F32).at[0, offset:offset + GDN_HEADS].set(vec.astype(F32))


def _col_pat(vec):
    return jnp.zeros((8, 1), F32).at[:GDN_HEADS, 0].set(vec.astype(F32))


def _pick(n, pref):
    return pref if n % pref == 0 else n


def kernel(x, p, positions, attn_norm, w_in, conv_w, a_log, dt_bias, gdn_norm, w_o_a, q_norm, k_norm,
           lambda_q1, lambda_k1, lambda_q2, lambda_k2, diff_norm, w_o_b, w_out, mlp_norm, w_up, w_down,
           ple_norm, w_ple_gate, w_ple):
    b, t, _ = x.shape
    m = b * t
    depth = w_in.shape[0]
    assert t % GDN_BLK == 0, "sequence length must be a multiple of the gdn_prep block"

    tm_big = _pick(m, 1024)
    tm_mid = _pick(m, 512)
    tq = _pick(t, 512)
    tk = _pick(t, 512)
    assert tk % tq == 0
    cb = _pick(t // CHUNK, 8)
    pos3 = positions.reshape(m // tk, 1, tk)
    freq = (ROPE_THETA ** (-jnp.arange(0, ROT_DIM, 2, dtype=F32) / ROT_DIM)).reshape(ROT_DIM // 2, 1)
    row = lambda v: v.astype(F32).reshape(1, -1)

    x2d = x.reshape(m, D_MODEL)
    for i in range(depth):
        lam_init = 0.8 - 0.6 * math.exp(-0.3 * i)
        w_main, w_gz = _split_w_in(w_in[i])
        proj = _in_proj(x2d, row(attn_norm[i]), w_main, tm_mid, 512)

        dqt, dk, dvt = _diff_prep(proj, pos3, freq, jnp.tile(row(q_norm[i]), (1, DIFF_QK // DIFF_DH)),
                                  jnp.tile(row(k_norm[i]), (1, DIFF_QK // DIFF_DH)), tk)
        o_b = _flash_diff(dqt, dk, dvt, row(lambda_q1[i]), row(lambda_k1[i]), row(lambda_q2[i]),
                          row(lambda_k2[i]), row(diff_norm[i]), b, t, tq, tk, lam_init)

        wq, u, aq, kdt, eg = _gdn_prep(proj, conv_w[i].astype(F32), _lane_pat(a_log[i], 0),
                                       _lane_pat(dt_bias[i], 0), _col_pat(a_log[i]), _col_pat(dt_bias[i]), b, t)
        o_a = _gdn_scan(wq, u, aq, kdt, eg, cb)

        x1 = _merge(x2d, row(attn_norm[i]), w_gz, o_a, o_b, row(gdn_norm[i]), w_o_a[i].astype(BF16),
                    w_o_b[i].astype(BF16), w_out[i].astype(BF16), t, _pick(t, 512))
        x2d = _mlp_ple(x1, row(mlp_norm[i]), w_up[i].astype(BF16), w_down[i].astype(BF16), row(ple_norm[i]),
                       w_ple_gate[i].astype(BF16), p, i, w_ple[i].astype(BF16), _pick(t, 1024), 1024)
    return x2d.reshape(b, t, D_MODEL)
```

```python
import functools
import math

import jax
import jax.numpy as jnp
from jax import lax
from jax.experimental import pallas as pl
from jax.experimental.pallas import tpu as pltpu

F32 = jnp.float32
BF16 = jnp.bfloat16

D_MODEL = 1024
PLE_DIM = 256
EPS = 1e-6
GDN_HEADS = 4
GDN_DK = 128
GDN_DV = 128
CONV_K = 4
CHUNK = 64
DIFF_HEADS = 4
DIFF_DH = 64
DIFF_DV = 2 * DIFF_DH
ROT_DIM = DIFF_DH // 4
ROPE_THETA = 500000.0
D_FF = 4 * D_MODEL

GDN_QK = GDN_HEADS * GDN_DK
GDN_V = GDN_HEADS * GDN_DV
DIFF_QK = DIFF_HEADS * 2 * DIFF_DH
DIFF_V = DIFF_HEADS * DIFF_DV

COL_QKV = 0
COL_DQ = 1536
COL_DK = 2048
COL_DV = 2560
COL_AB = 3072
N_PAD = COL_AB + 128
N_GZ = GDN_V + 2 * D_MODEL

LANE = 128
GDN_BLK = 512
CPB = GDN_BLK // CHUNK
FLASH_HPB = 4
LOG2E = math.log2(math.e)
ACC_ROWS = DIFF_DV + 16
QK_LAG = 0
PV_LAG = 1
INV_BASE = 16
VMEM_LIMIT = 48 * 1024 * 1024
FLASH_VMEM_LIMIT = 58 * 1024 * 1024


def _cparams(sem):
    return pltpu.CompilerParams(dimension_semantics=sem, vmem_limit_bytes=VMEM_LIMIT)


def _dot(a, b):
    return jnp.dot(a, b, preferred_element_type=F32)


def _dot_nt(a, b):
    return lax.dot_general(a, b, (((1,), (1,)), ((), ())), preferred_element_type=F32)


def _split_bf16(a):
    hi = a.astype(BF16)
    lo = (a - hi.astype(F32)).astype(BF16)
    return hi, lo


def _tdot(a, b):
    return _dot(a.astype(BF16), b.astype(BF16))


def _sigmoid(x):
    return 1.0 / (1.0 + jnp.exp(-x))


def _silu(x):
    return x * _sigmoid(x)


def _softplus(x):
    return jnp.maximum(x, 0.0) + jnp.log(1.0 + jnp.exp(-jnp.abs(x)))


def _inproj_kernel(x_ref, g_ref, w_ref, o_ref, *, tn):
    x = x_ref[...]
    ms = jnp.mean(x * x, axis=-1, keepdims=True)
    h = (x * lax.rsqrt(ms + EPS) * g_ref[...]).astype(BF16)
    for lo in range(0, N_PAD, tn):
        hi = min(lo + tn, N_PAD)
        o_ref[:, lo:hi] = _dot(h, w_ref[:, lo:hi])


def _in_proj(x2d, gain, w_bf16, tm, tn):
    m = x2d.shape[0]
    return pl.pallas_call(
        functools.partial(_inproj_kernel, tn=tn),
        grid=(m // tm,),
        in_specs=[
            pl.BlockSpec((tm, D_MODEL), lambda i: (i, 0)),
            pl.BlockSpec((1, D_MODEL), lambda i: (0, 0)),
            pl.BlockSpec((D_MODEL, N_PAD), lambda i: (0, 0), pipeline_mode=pl.Buffered(1)),
        ],
        out_specs=pl.BlockSpec((tm, N_PAD), lambda i: (i, 0)),
        out_shape=jax.ShapeDtypeStruct((m, N_PAD), F32),
        compiler_params=_cparams(("parallel",)),
        name="in_proj",
    )(x2d, gain, w_bf16)


def _diffprep_kernel(pos_ref, freq_ref, q_ref, k_ref, v_ref, qg_ref, kg_ref, oq_ref, ok_ref, ov_ref):
    tm = q_ref.shape[0]
    ang = freq_ref[...] * pos_ref[0].astype(F32)
    cos_t, sin_t = jnp.cos(ang), jnp.sin(ang)
    half = ROT_DIM // 2
    fi = jnp.bitwise_and(lax.broadcasted_iota(jnp.int32, (3 * half, LANE), 0), half - 1)
    d = jnp.bitwise_and(lax.broadcasted_iota(jnp.int32, (3 * half, LANE), 1), DIFF_DH - 1)
    e_lo = jnp.where(d == fi, 1.0, 0.0).astype(BF16)
    e_hi = jnp.where(d == fi + half, 1.0, 0.0).astype(BF16)

    def expand(table_t, sel):
        p0 = table_t.astype(BF16).astype(F32)
        r1 = table_t - p0
        p1 = r1.astype(BF16).astype(F32)
        p2 = (r1 - p1).astype(BF16).astype(F32)
        pieces = jnp.concatenate([p0, p1, p2], axis=0).astype(BF16)
        out = lax.dot_general(pieces, sel, (((0,), (0,)), ((), ())), preferred_element_type=F32)
        return jnp.concatenate([out] * (DIFF_QK // LANE), axis=1)

    lane = lax.broadcasted_iota(jnp.int32, (1, DIFF_QK), 1)
    passthrough = jnp.where(jnp.bitwise_and(lane, DIFF_DH - 1) >= ROT_DIM, 1.0, 0.0).astype(F32)
    cos_f = expand(cos_t, e_lo + e_hi) + passthrough
    sin_a = -expand(sin_t, e_lo)
    sin_b = expand(sin_t, e_hi)

    ri = lax.broadcasted_iota(jnp.int32, (DIFF_QK, DIFF_QK), 0)
    ci = lax.broadcasted_iota(jnp.int32, (DIFF_QK, DIFF_QK), 1)
    grp = jnp.where(jnp.right_shift(ri, 6) == jnp.right_shift(ci, 6), 1.0 / DIFF_DH, 0.0).astype(BF16)

    def norm_rot(x, gain):
        sq = x * x
        hi, lo = _split_bf16(sq)
        ms = _dot(hi, grp) + _dot(lo, grp)
        y = x * lax.rsqrt(ms + EPS) * gain
        up = pltpu.roll(y, DIFF_QK - half, 1)
        dn = pltpu.roll(y, half, 1)
        return y * cos_f + up * sin_a + dn * sin_b

    oq_ref[0] = norm_rot(q_ref[...], qg_ref[...] * (DIFF_DH ** -0.5 * LOG2E)).T.astype(BF16)
    ok_ref[...] = norm_rot(k_ref[...], kg_ref[...]).astype(BF16)
    ov_ref[0] = v_ref[...].T.astype(BF16)


def _diff_prep(proj, pos3, freq, qg, kg, tm):
    m = proj.shape[0]
    blk = lambda c: pl.BlockSpec((tm, DIFF_QK), lambda i: (i, c))
    small = lambda s: pl.BlockSpec(s, lambda i: (0,) * len(s))
    out_t = jax.ShapeDtypeStruct((m // tm, DIFF_QK, tm), BF16)
    spec_t = pl.BlockSpec((1, DIFF_QK, tm), lambda i: (i, 0, 0))
    return pl.pallas_call(
        _diffprep_kernel,
        grid=(m // tm,),
        in_specs=[
            pl.BlockSpec((1, 1, tm), lambda i: (i, 0, 0)),
            small((ROT_DIM // 2, 1)),
            blk(COL_DQ // DIFF_QK), blk(COL_DK // DIFF_QK), blk(COL_DV // DIFF_QK),
            small((1, DIFF_QK)), small((1, DIFF_QK)),
        ],
        out_specs=[spec_t, pl.BlockSpec((tm, DIFF_QK), lambda i: (i, 0)), spec_t],
        out_shape=[out_t, jax.ShapeDtypeStruct((m, DIFF_QK), BF16), out_t],
        compiler_params=_cparams(("parallel",)),
        name="diff_prep",
    )(pos3, freq, proj, proj, proj, qg, kg)


def _flash_kernel(qt_ref, k_ref, vt_ref, lq1_ref, lk1_ref, lq2_ref, lk2_ref, gain_ref, o_ref,
                  *scratch, tq, tk, lam_init):
    qi = pl.program_id(2)
    chains = [(hh, c) for hh in range(FLASH_HPB) for c in range(2)]
    acc_scr, s_scr, p_scr = (scratch[k * len(chains):(k + 1) * len(chains)] for k in range(3))
    qst = {}
    for hh in range(FLASH_HPB):
        qt = qt_ref[0, hh * LANE:(hh + 1) * LANE, :]
        row = lax.broadcasted_iota(jnp.int32, qt.shape, 0)
        zero = jnp.zeros_like(qt)
        qst[hh, 0] = jnp.where(row < DIFF_DH, qt, zero)
        qst[hh, 1] = jnp.where(row >= DIFF_DH, qt, zero)

    nch = len(chains)
    last = nch - 1
    for c in range(nch):
        acc_scr[c][...] = jnp.zeros((ACC_ROWS, tq), F32)
    ones_rows = jnp.ones((ACC_ROWS - DIFF_DV, tk), BF16)

    def qk(c, j):
        hh, comp = chains[c]
        start = pl.multiple_of(j * tk, tk)
        s = _dot(k_ref[pl.ds(start, tk), hh * LANE:(hh + 1) * LANE], qst[hh, comp])
        s_scr[c][...] = s
        return jnp.max(s, axis=0, keepdims=True)

    def pv(c, j, alpha):
        hh, comp = chains[c]
        vt_aug = jnp.concatenate([vt_ref[j, hh * LANE:(hh + 1) * LANE, :], ones_rows], axis=0)
        upd = _dot(vt_aug, p_scr[c][...])
        acc_scr[c][...] = alpha * acc_scr[c][...] + upd

    def softmax(c, j, m_old, cmax, masked):
        s = s_scr[c][...]
        if masked:
            r = lax.broadcasted_iota(jnp.int32, (tk, tq), 0)
            col = lax.broadcasted_iota(jnp.int32, (tk, tq), 1)
            s = jnp.where(j * tk + r <= qi * tq + col, s, -jnp.inf)
            cmax = jnp.max(s, axis=0, keepdims=True)
        m_new = jnp.maximum(m_old, cmax)
        alpha = jnp.exp2(m_old - m_new)
        p_scr[c][...] = jnp.exp2(s - m_new).astype(BF16)
        return m_new, alpha

    n_full = (qi * tq) // tk
    n_early = nch - max(QK_LAG, 1)
    cmax0 = tuple(qk(c, 0) for c in range(n_early))
    for c in range(nch - PV_LAG, nch):
        p_scr[c][...] = jnp.zeros((tk, tq), BF16)

    def trip(i, carry):
        stats, cmaxs, alpha_tail = carry
        stats, cmaxs = list(stats), list(cmaxs)
        cmaxs += [qk(c, i) for c in range(n_early, nch)]
        for k, c in enumerate(range(nch - PV_LAG, nch)):
            pv(c, jnp.maximum(i - 1, 0), alpha_tail[k])
        alphas = [None] * nch
        for c in range(nch):
            stats[c], alphas[c] = softmax(c, i, stats[c], cmaxs[c], False)
            if 0 <= c - QK_LAG < n_early:
                cmaxs[c - QK_LAG] = qk(c - QK_LAG, i + 1)
            if c >= PV_LAG:
                pv(c - PV_LAG, i, alphas[c - PV_LAG])
        return tuple(stats), tuple(cmaxs[:n_early]), tuple(alphas[nch - PV_LAG:])

    init = (tuple(jnp.full((1, tq), -jnp.inf, F32) for _ in chains), cmax0,
            tuple(jnp.ones((1, tq), F32) for _ in range(PV_LAG)))
    stats, _, alpha_tail = lax.fori_loop(0, n_full, trip, init)
    stats = list(stats)
    for c in range(n_early, nch):
        qk(c, n_full)
    for k, c in enumerate(range(nch - PV_LAG, nch)):
        pv(c, jnp.maximum(n_full - 1, 0), alpha_tail[k])
    for c in range(nch):
        stats[c], alpha = softmax(c, n_full, stats[c], None, True)
        pv(c, n_full, alpha)

    lam = (jnp.exp(jnp.sum(lq1_ref[...] * lk1_ref[...], axis=-1, keepdims=True))
           - jnp.exp(jnp.sum(lq2_ref[...] * lk2_ref[...], axis=-1, keepdims=True)) + lam_init)
    for hh in range(FLASH_HPB):
        a1 = acc_scr[2 * hh][...]
        a2 = acc_scr[2 * hh + 1][...]
        o1 = a1[:DIFF_DV] / a1[DIFF_DV:DIFF_DV + 1]
        o2 = a2[:DIFF_DV] / a2[DIFF_DV:DIFF_DV + 1]
        o = (o1 - lam * o2).T
        ms = jnp.mean(o * o, axis=-1, keepdims=True)
        o_ref[:, hh * LANE:(hh + 1) * LANE] = (
            o * lax.rsqrt(ms + EPS) * gain_ref[...] * (1.0 - lam_init)).astype(o_ref.dtype)


def _flash_diff(dqt, dk, dvt, lq1, lk1, lq2, lk2, gain, b, t, tq, tk, lam_init):
    nq = t // tq
    nkb = t // tk
    qpb = tk // tq
    hw = FLASH_HPB * LANE
    small = lambda s: pl.BlockSpec(s, lambda bi, h, qi: (0,) * len(s))
    return pl.pallas_call(
        functools.partial(_flash_kernel, tq=tq, tk=tk, lam_init=lam_init),
        grid=(b, DIFF_HEADS // FLASH_HPB, nq),
        in_specs=[
            pl.BlockSpec((1, hw, tq), lambda bi, h, qi: (bi * nkb + qi // qpb, h, qi % qpb)),
            pl.BlockSpec((t, hw), lambda bi, h, qi: (bi, h)),
            pl.BlockSpec((nkb, hw, tk), lambda bi, h, qi: (bi, h, 0)),
            small((1, DIFF_DH)), small((1, DIFF_DH)), small((1, DIFF_DH)), small((1, DIFF_DH)),
            small((1, DIFF_DV)),
        ],
        out_specs=pl.BlockSpec((tq, hw), lambda bi, h, qi: (bi * nq + qi, h)),
        out_shape=jax.ShapeDtypeStruct((b * t, DIFF_V), BF16),
        scratch_shapes=([pltpu.VMEM((ACC_ROWS, tq), F32)] * (2 * FLASH_HPB)
                        + [pltpu.VMEM((tk, tq), F32)] * (2 * FLASH_HPB)
                        + [pltpu.VMEM((tk, tq), BF16)] * (2 * FLASH_HPB)),
        compiler_params=pltpu.CompilerParams(dimension_semantics=("parallel", "parallel", "arbitrary"),
                                             vmem_limit_bytes=FLASH_VMEM_LIMIT),
        name="flash_diff",
    )(dqt, dk, dvt, lq1, lk1, lq2, lk2, gain)


def _gdnprep_kernel(cur_ref, halo_ref, ab_ref, convw_ref, alog_ref, dtb_ref, alogc_ref, dtbc_ref,
                    wq_ref, u_ref, aq_ref, kdt_ref, eg_ref, *, nblk):
    i = pl.program_id(0)
    n = GDN_BLK
    keep = jnp.where(i % nblk == 0, 0.0, 1.0).astype(F32)
    xs = jnp.concatenate([halo_ref[...] * keep, cur_ref[...]], axis=0)
    cw = convw_ref[...]
    conv = xs[8:8 + n] * cw[CONV_K - 1:CONV_K]
    for j in range(CONV_K - 1):
        off = 8 - (CONV_K - 1) + j
        conv = conv + xs[off:off + n] * cw[j:j + 1]
    qkv = _silu(conv)

    ab = ab_ref[...]
    ab_t = ab.T
    g_col = -jnp.exp(alog_ref[...]) * _softplus(ab + dtb_ref[...])
    beta_col = _sigmoid(ab)
    g_row = -jnp.exp(alogc_ref[...]) * _softplus(ab_t[:8] + dtbc_ref[...])

    sb_n = LANE
    ri = lax.broadcasted_iota(jnp.int32, (sb_n, sb_n), 0)
    ci = lax.broadcasted_iota(jnp.int32, (sb_n, sb_n), 1)
    same = jnp.right_shift(ri, 6) == jnp.right_shift(ci, 6)
    same32 = jnp.right_shift(ri, 5) == jnp.right_shift(ci, 5)
    same16 = jnp.right_shift(ri, 4) == jnp.right_shift(ci, 4)
    incl = jnp.logical_and(same, ri >= ci)
    strict = jnp.logical_and(same, ri > ci)
    one = jnp.ones((sb_n, sb_n), F32)
    zero = jnp.zeros((sb_n, sb_n), F32)
    tri = jnp.where(incl, one, zero).astype(BF16)
    blk_ones = jnp.where(same, one, zero).astype(BF16)
    eye = jnp.where(ri == ci, one, zero)

    def split3(a):
        p0 = a.astype(BF16)
        r1 = a - p0.astype(F32)
        p1 = r1.astype(BF16)
        return p0, p1, (r1 - p1.astype(F32)).astype(BF16)

    units = [(h, sb) for sb in range(n // sb_n) for h in range(GDN_HEADS)]
    gc_cols, gl_cols, gc_rows = [], [], []
    for sb in range(n // sb_n):
        rows = slice(sb * sb_n, (sb + 1) * sb_n)
        pc = split3(g_col[rows])
        pr = split3(g_row[:, rows])
        gc_cols.append(sum(_dot(tri, p) for p in pc))
        gl_cols.append(sum(_dot(blk_ones, p) for p in pc))
        gc_rows.append(sum(_dot_nt(p, tri) for p in pr))

    qs, ks, kbs, kbfs, decays, egcs, glbs, gcbs, rhss = [], [], [], [], [], [], [], [], []
    for h, sb in units:
        rows = slice(sb * sb_n, (sb + 1) * sb_n)
        q = qkv[rows, h * GDN_DK:(h + 1) * GDN_DK]
        k = qkv[rows, GDN_QK + h * GDN_DK: GDN_QK + (h + 1) * GDN_DK]
        v = qkv[rows, 2 * GDN_QK + h * GDN_DV: 2 * GDN_QK + (h + 1) * GDN_DV]
        q = q * lax.rsqrt(jnp.sum(q * q, axis=-1, keepdims=True) + EPS) * (GDN_DK ** -0.5)
        k = k * lax.rsqrt(jnp.sum(k * k, axis=-1, keepdims=True) + EPS)
        beta = beta_col[rows, GDN_HEADS + h: GDN_HEADS + h + 1]
        gcb = jnp.broadcast_to(gc_cols[sb][:, h:h + 1], (sb_n, GDN_DK))
        glb = jnp.broadcast_to(gl_cols[sb][:, h:h + 1], (sb_n, GDN_DK))
        gdiff = gcb - gc_rows[sb][h:h + 1, :]
        kb = k * beta
        egc = jnp.exp(gcb)
        qs.append(q); ks.append(k); kbs.append(kb); kbfs.append(k.astype(BF16))
        decays.append(jnp.where(incl, jnp.exp(jnp.where(incl, gdiff, zero)), zero))
        egcs.append(egc); glbs.append(glb); gcbs.append(gcb)
        rhss.append(jnp.concatenate([v * beta, kb * egc], axis=1).astype(BF16))

    nu = range(len(units))
    kq = [_dot_nt(jnp.concatenate([kbs[i], qs[i]], axis=0).astype(BF16), kbfs[i]) for i in nu]
    lowers = [jnp.where(strict, kq[i][:sb_n] * decays[i], zero) for i in nu]
    xs_ = [jnp.where(same16, -lowers[i], zero) for i in nu]
    tmats = [eye + xs_[i] for i in nu]
    for _ in range(int(math.log2(INV_BASE)) - 1):
        xs_ = [_tdot(xs_[i], xs_[i]) for i in nu]
        tmats = [tmats[i] + _tdot(tmats[i], xs_[i]) for i in nu]
    for lo_mask, hi_mask in ((same16, same32), (same32, same)):
        sel = jnp.logical_and(hi_mask, jnp.logical_not(lo_mask))
        tmp = [_tdot(tmats[i], jnp.where(sel, lowers[i], zero)) for i in nu]
        tmats = [tmats[i] - _tdot(tmp[i], tmats[i]) for i in nu]
    uws = [_dot(tmats[i].astype(BF16), rhss[i]) for i in nu]

    cps = sb_n // CHUNK
    for i, (h, sb) in enumerate(units):
        u = uws[i][:, :GDN_DV]
        w = uws[i][:, GDN_DV:].astype(BF16)
        a_intra = (kq[i][sb_n:] * decays[i]).astype(BF16)
        q_dec = (qs[i] * egcs[i]).astype(BF16)
        kd_t = (ks[i] * jnp.exp(glbs[i] - gcbs[i])).T.astype(BF16)
        eg = jnp.exp(glbs[i])
        for cc in range(cps):
            c = sb * cps + cc
            rs = slice(cc * CHUNK, (cc + 1) * CHUNK)
            wq_ref[0, h, c, 0:CHUNK, :] = w[rs]
            wq_ref[0, h, c, CHUNK:2 * CHUNK, :] = q_dec[rs]
            u_ref[0, h, c] = u[rs]
            aq_ref[0, h, c] = a_intra[rs, rs]
            kdt_ref[0, h, c] = kd_t[:, rs]
            eg_ref[0, h, c] = eg[cc * CHUNK: cc * CHUNK + 1, :]


def _gdn_prep(proj, conv_w, a_log_pat, dtb_pat, a_log_col, dtb_col, b, t):
    nblk = t // GDN_BLK
    nc = t // CHUNK
    c3 = 2 * GDN_QK + GDN_V
    small = lambda s: pl.BlockSpec(s, lambda i: (0,) * len(s))
    omap = lambda i: (i // nblk, 0, i % nblk, 0, 0)
    return pl.pallas_call(
        functools.partial(_gdnprep_kernel, nblk=nblk),
        grid=(b * nblk,),
        in_specs=[
            pl.BlockSpec((GDN_BLK, c3), lambda i: (i, COL_QKV // c3)),
            pl.BlockSpec((8, c3), lambda i: (jnp.maximum(i * (GDN_BLK // 8) - 1, 0), COL_QKV // c3)),
            pl.BlockSpec((GDN_BLK, LANE), lambda i: (i, COL_AB // LANE)),
            small((CONV_K, c3)), small((1, LANE)), small((1, LANE)), small((8, 1)), small((8, 1)),
        ],
        out_specs=[
            pl.BlockSpec((1, GDN_HEADS, CPB, 2 * CHUNK, GDN_DK), omap),
            pl.BlockSpec((1, GDN_HEADS, CPB, CHUNK, GDN_DV), omap),
            pl.BlockSpec((1, GDN_HEADS, CPB, CHUNK, CHUNK), omap),
            pl.BlockSpec((1, GDN_HEADS, CPB, GDN_DK, CHUNK), omap),
            pl.BlockSpec((1, GDN_HEADS, CPB, 1, GDN_DV), omap),
        ],
        out_shape=[
            jax.ShapeDtypeStruct((b, GDN_HEADS, nc, 2 * CHUNK, GDN_DK), BF16),
            jax.ShapeDtypeStruct((b, GDN_HEADS, nc, CHUNK, GDN_DV), F32),
            jax.ShapeDtypeStruct((b, GDN_HEADS, nc, CHUNK, CHUNK), BF16),
            jax.ShapeDtypeStruct((b, GDN_HEADS, nc, GDN_DK, CHUNK), BF16),
            jax.ShapeDtypeStruct((b, GDN_HEADS, nc, 1, GDN_DV), F32),
        ],
        compiler_params=_cparams(("parallel",)),
        name="gdn_prep",
    )(proj, proj, proj, conv_w, a_log_pat, dtb_pat, a_log_col, dtb_col)


def _gdnscan_kernel(wq_ref, u_ref, aq_ref, kdt_ref, eg_ref, o_ref, s_scr, *, nb, cb):
    @pl.when(pl.program_id(0) == 0)
    def _():
        s_scr[...] = jnp.zeros(s_scr.shape, F32)

    streams = [(bi, h) for bi in range(nb) for h in range(GDN_HEADS)]

    def chunk(c, carry):
        rs = [_dot(wq_ref[bi, h, c], s_scr[bi, h].astype(BF16)) for bi, h in streams]
        vbs = [(u_ref[bi, h, c] - r[:CHUNK]).astype(BF16) for (bi, h), r in zip(streams, rs)]
        rows = pl.ds(pl.multiple_of(c * CHUNK, CHUNK), CHUNK)
        for (bi, h), r, vb in zip(streams, rs, vbs):
            o_ref[bi, h, rows, :] = r[CHUNK:] + _dot(aq_ref[bi, h, c], vb)
        for (bi, h), vb in zip(streams, vbs):
            s_scr[bi, h] = s_scr[bi, h] * eg_ref[bi, h, c] + _dot(kdt_ref[bi, h, c], vb)
        return carry

    lax.fori_loop(0, cb, chunk, 0)


def _gdn_scan(wq, u, aq, kdt, eg, cb):
    b, _, nc = wq.shape[:3]
    spec = lambda d0, d1: pl.BlockSpec((b, GDN_HEADS, cb, d0, d1), lambda i: (0, 0, i, 0, 0))
    return pl.pallas_call(
        functools.partial(_gdnscan_kernel, nb=b, cb=cb),
        grid=(nc // cb,),
        in_specs=[spec(2 * CHUNK, GDN_DK), spec(CHUNK, GDN_DV), spec(CHUNK, CHUNK),
                  spec(GDN_DK, CHUNK), spec(1, GDN_DV)],
        out_specs=pl.BlockSpec((b, GDN_HEADS, cb * CHUNK, GDN_DV), lambda i: (0, 0, i, 0)),
        out_shape=jax.ShapeDtypeStruct((b, GDN_HEADS, nc * CHUNK, GDN_DV), F32),
        scratch_shapes=[pltpu.VMEM((b, GDN_HEADS, GDN_DK, GDN_DV), F32)],
        compiler_params=_cparams(("arbitrary",)),
        name="gdn_scan",
    )(wq, u, aq, kdt, eg)


def _merge_kernel(x_ref, an_ref, wgz_ref, oa_ref, ob_ref, gn_ref, woa_ref, wob_ref, wout_ref, o_ref):
    x = x_ref[...]
    ms = jnp.mean(x * x, axis=-1, keepdims=True)
    hn = (x * lax.rsqrt(ms + EPS) * an_ref[...]).astype(BF16)
    gz = _dot(hn, wgz_ref[...])
    parts = []
    for h in range(GDN_HEADS):
        o = oa_ref[0, h]
        ms = jnp.mean(o * o, axis=-1, keepdims=True)
        z = gz[:, h * GDN_DV:(h + 1) * GDN_DV]
        parts.append((o * lax.rsqrt(ms + EPS) * gn_ref[...] * _silu(z)).astype(BF16))
    y_a = _dot(jnp.concatenate(parts, axis=1), woa_ref[...])
    y_b = _dot(ob_ref[...], wob_ref[...])
    merged = (_sigmoid(gz[:, GDN_V:GDN_V + D_MODEL]) * y_a + _sigmoid(gz[:, GDN_V + D_MODEL:]) * y_b)
    o_ref[...] = x + _dot(merged.astype(BF16), wout_ref[...])


def _merge(x2d, attn_gain, w_gz, o_a, o_b, gdn_gain, w_o_a, w_o_b, w_out, t, tm):
    m = x2d.shape[0]
    npb = t // tm
    small = lambda s: pl.BlockSpec(s, lambda i: (0,) * len(s), pipeline_mode=pl.Buffered(1))
    return pl.pallas_call(
        _merge_kernel,
        grid=(m // tm,),
        in_specs=[
            pl.BlockSpec((tm, D_MODEL), lambda i: (i, 0)),
            small((1, D_MODEL)), small((D_MODEL, N_GZ)),
            pl.BlockSpec((1, GDN_HEADS, tm, GDN_DV), lambda i: (i // npb, 0, i % npb, 0)),
            pl.BlockSpec((tm, DIFF_V), lambda i: (i, 0)),
            small((1, GDN_DV)), small((GDN_V, D_MODEL)), small((DIFF_V, D_MODEL)), small((D_MODEL, D_MODEL)),
        ],
        out_specs=pl.BlockSpec((tm, D_MODEL), lambda i: (i, 0)),
        out_shape=jax.ShapeDtypeStruct((m, D_MODEL), F32),
        compiler_params=_cparams(("parallel",)),
        name="merge",
    )(x2d, attn_gain, w_gz, o_a, o_b, gdn_gain, w_o_a, w_o_b, w_out)


def _mlp_kernel(x_ref, mg_ref, wup_ref, wdn_ref, pg_ref, wpg_ref, p_ref, wp_ref, o_ref, h_scr, acc_scr):
    f = pl.program_id(1)

    @pl.when(f == 0)
    def _():
        x = x_ref[...]
        ms = jnp.mean(x * x, axis=-1, keepdims=True)
        h_scr[...] = (x * lax.rsqrt(ms + EPS) * mg_ref[...]).astype(BF16)
        acc_scr[...] = jnp.zeros(acc_scr.shape, F32)

    a = jnp.maximum(_dot(h_scr[...], wup_ref[...]), 0.0)
    acc_scr[...] += _dot((a * a).astype(BF16), wdn_ref[...])

    @pl.when(f == pl.num_programs(1) - 1)
    def _():
        x2 = x_ref[...] + acc_scr[...]
        ms = jnp.mean(x2 * x2, axis=-1, keepdims=True)
        hn = (x2 * lax.rsqrt(ms + EPS) * pg_ref[...]).astype(BF16)
        gate = _sigmoid(_dot(hn, wpg_ref[...]))
        o_ref[...] = x2 + gate * _dot(p_ref[0, 0].astype(BF16), wp_ref[...])


def _mlp_ple(x1, mlp_gain, w_up, w_down, ple_gain, w_ple_gate, p, layer, w_ple, tm, tf):
    m = x1.shape[0]
    npb = p.shape[2] // tm
    small = lambda s: pl.BlockSpec(s, lambda i, f: (0,) * len(s))
    return pl.pallas_call(
        _mlp_kernel,
        grid=(m // tm, D_FF // tf),
        in_specs=[
            pl.BlockSpec((tm, D_MODEL), lambda i, f: (i, 0)),
            small((1, D_MODEL)),
            pl.BlockSpec((D_MODEL, tf), lambda i, f: (0, f)),
            pl.BlockSpec((tf, D_MODEL), lambda i, f: (f, 0)),
            small((1, D_MODEL)), small((D_MODEL, D_MODEL)),
            pl.BlockSpec((1, 1, tm, PLE_DIM), lambda i, f: (layer, i // npb, i % npb, 0)),
            small((PLE_DIM, D_MODEL)),
        ],
        out_specs=pl.BlockSpec((tm, D_MODEL), lambda i, f: (i, 0)),
        out_shape=jax.ShapeDtypeStruct((m, D_MODEL), F32),
        scratch_shapes=[pltpu.VMEM((tm, D_MODEL), BF16), pltpu.VMEM((tm, D_MODEL), F32)],
        compiler_params=_cparams(("parallel", "arbitrary")),
        name="mlp_ple",
    )(x1, mlp_gain, w_up, w_down, ple_gain, w_ple_gate, p, w_ple)


def _split_w_in(w):
    o_z = 3 * GDN_QK
    o_a = o_z + GDN_V
    o_dq = o_a + 2 * GDN_HEADS
    o_gate = o_dq + 3 * DIFF_QK
    pad = jnp.zeros((D_MODEL, N_PAD - COL_AB - 2 * GDN_HEADS), w.dtype)
    w_main = jnp.concatenate([w[:, :o_z], w[:, o_dq:o_gate], w[:, o_a:o_dq], pad], axis=1).astype(BF16)
    w_gz = jnp.concatenate([w[:, o_z:o_a], w[:, o_gate:]], axis=1).astype(BF16)
    return w_main, w_gz


def _lane_pat(vec, offset):
    return jnp.zeros((1, LANE), F32).at[0, offset:offset + GDN_HEADS].set(vec.astype(F32))


def _col_pat(vec):
    return jnp.zeros((8, 1), F32).at[:GDN_HEADS, 0].set(vec.astype(F32))


def _pick(n, pref):
    return pref if n % pref == 0 else n


def kernel(x, p, positions, attn_norm, w_in, conv_w, a_log, dt_bias, gdn_norm, w_o_a, q_norm, k_norm,
           lambda_q1, lambda_k1, lambda_q2, lambda_k2, diff_norm, w_o_b, w_out, mlp_norm, w_up, w_down,
           ple_norm, w_ple_gate, w_ple):
    b, t, _ = x.shape
    m = b * t
    depth = w_in.shape[0]
    assert t % GDN_BLK == 0, "sequence length must be a multiple of the gdn_prep block"

    tm_mid = _pick(m, 512)
    tq = _pick(t, 512)
    tk = _pick(t, 512)
    assert tk % tq == 0
    cb = _pick(t // CHUNK, 8)
    pos3 = positions.reshape(m // tk, 1, tk)
    freq = (ROPE_THETA ** (-jnp.arange(0, ROT_DIM, 2, dtype=F32) / ROT_DIM)).reshape(ROT_DIM // 2, 1)
    row = lambda v: v.astype(F32).reshape(1, -1)

    x2d = x.reshape(m, D_MODEL)
    for i in range(depth):
        lam_init = 0.8 - 0.6 * math.exp(-0.3 * i)
        w_main, w_gz = _split_w_in(w_in[i])
        proj = _in_proj(x2d, row(attn_norm[i]), w_main, tm_mid, 512)

        dqt, dk, dvt = _diff_prep(proj, pos3, freq, jnp.tile(row(q_norm[i]), (1, DIFF_QK // DIFF_DH)),
                                  jnp.tile(row(k_norm[i]), (1, DIFF_QK // DIFF_DH)), tk)
        o_b = _flash_diff(dqt, dk, dvt, row(lambda_q1[i]), row(lambda_k1[i]), row(lambda_q2[i]),
                          row(lambda_k2[i]), row(diff_norm[i]), b, t, tq, tk, lam_init)

        wq, u, aq, kdt, eg = _gdn_prep(proj, conv_w[i].astype(F32), _lane_pat(a_log[i], 0),
                                       _lane_pat(dt_bias[i], 0), _col_pat(a_log[i]), _col_pat(dt_bias[i]), b, t)
        o_a = _gdn_scan(wq, u, aq, kdt, eg, cb)

        x1 = _merge(x2d, row(attn_norm[i]), w_gz, o_a, o_b, row(gdn_norm[i]), w_o_a[i].astype(BF16),
                    w_o_b[i].astype(BF16), w_out[i].astype(BF16), t, _pick(t, 512))
        x2d = _mlp_ple(x1, row(mlp_norm[i]), w_up[i].astype(BF16), w_down[i].astype(BF16), row(ple_norm[i]),
                       w_ple_gate[i].astype(BF16), p, i, w_ple[i].astype(BF16), _pick(t, 1024), 1024)
    return x2d.reshape(b, t, D_MODEL)
```
